```python
import jax, jax.numpy as jnp
from jax import lax
import numpy as np

D_MODEL = 1024
BATCH = 4
SEQ = 4096
DEPTH = 2
DEC_BATCH = 16
DEC_SEQ = 16
PAST_LEN = 2048

CHUNK = 64
N_MIXERS = 2
N_POOL_LAYERS = (DEPTH + 1) // 2
N_GLA_LAYERS = DEPTH // 2
POOL_WINDOWS = (2, 4, 8, 16)
N_POOL_GROUPS = len(POOL_WINDOWS)
POOL_GROUP = D_MODEL // N_POOL_GROUPS
POOL_HIST = max(POOL_WINDOWS) - 1
GLA_HEADS = 4
GLA_KEY_DIM = D_MODEL // 2
GLA_VAL_DIM = D_MODEL
GLA_DK = GLA_KEY_DIM // GLA_HEADS
GLA_DV = GLA_VAL_DIM // GLA_HEADS
GLA_GATE_RANK = 16
GLA_GATE_NORMALIZER = 16.0
GLA_IN = 2 * GLA_KEY_DIM + 2 * GLA_VAL_DIM + GLA_GATE_RANK
D_FF = -(-8 * D_MODEL // (3 * 256)) * 256
PLE_DIM = 256
EPS = 1e-6

kernel_name = "hybrid_pool_gla_streaming_step"


def rmsnorm(x, g):
    xf = x.astype(jnp.float32)
    y = xf * lax.rsqrt(jnp.mean(xf * xf, axis=-1, keepdims=True) + EPS)
    return (y * g.astype(jnp.float32)).astype(x.dtype)


def pool_mixer(xn, hist, start, w_pool, b_pool, scale):
    B, T, D = xn.shape
    full = jnp.concatenate([hist.astype(xn.dtype), xn], axis=1).astype(jnp.float32)
    csum = jnp.concatenate([jnp.zeros((B, 1, D), jnp.float32), jnp.cumsum(full, axis=1)], axis=1)
    pos = start + jnp.arange(T)
    diffs = []
    for gi, w in enumerate(POOL_WINDOWS):
        sl = slice(gi * POOL_GROUP, (gi + 1) * POOL_GROUP)
        hi = csum[:, POOL_HIST + 1:, sl]
        lo = csum[:, POOL_HIST + 1 - w:POOL_HIST + 1 - w + T, sl]
        cnt = jnp.minimum(w, pos + 1).astype(jnp.float32)
        diffs.append((hi - lo) / cnt[None, :, None] - full[:, POOL_HIST:, sl])
    d = jnp.stack(diffs, axis=2)
    y = jnp.einsum('btgc,gcd->btgd', d, w_pool.astype(jnp.float32)).reshape(B, T, D)
    y = (y + b_pool.astype(jnp.float32)) * scale.astype(jnp.float32)
    return y.astype(xn.dtype), full[:, -POOL_HIST:].astype(xn.dtype)


def gla_recurrence(q, k, v, log_a, s0):
    B, T = q.shape[:2]
    L = min(CHUNK, T)
    nb = T // L
    mask = jnp.tril(jnp.ones((L, L), dtype=bool))

    def to_blocks(t):
        return t.reshape(B, nb, L, *t.shape[2:]).swapaxes(0, 1)

    def step(S, xs):
        qb, kb, vb, gb = xs
        b = jnp.cumsum(gb, axis=1)
        q_dec = qb * jnp.exp(b)
        k_inv = kb * jnp.exp(-b)
        scores = jnp.where(mask, jnp.einsum('blhk,bmhk->bhlm', q_dec, k_inv), 0.0)
        o = jnp.einsum('bhlm,bmhv->blhv', scores, vb) + jnp.einsum('blhk,bhkv->blhv', q_dec, S)
        b_last = b[:, -1]
        k_end = kb * jnp.exp(b_last[:, None] - b)
        S_new = jnp.exp(b_last)[..., None] * S + jnp.einsum('blhk,blhv->bhkv', k_end, vb)
        return S_new, o

    S_fin, o = lax.scan(step, s0, (to_blocks(q), to_blocks(k), to_blocks(v), to_blocks(log_a)))
    return o.swapaxes(0, 1).reshape(B, T, GLA_HEADS, GLA_DV), S_fin


def gla_mixer(xn, s0, w_in, w_gate_up, b_gate, norm_w, w_out):
    B, T, _ = xn.shape
    proj = xn @ w_in
    q, k, v, g, gr = jnp.split(proj, [GLA_KEY_DIM, 2 * GLA_KEY_DIM, 2 * GLA_KEY_DIM + GLA_VAL_DIM,
                                      2 * GLA_KEY_DIM + 2 * GLA_VAL_DIM], axis=-1)
    log_a = jax.nn.log_sigmoid((gr @ w_gate_up + b_gate).astype(jnp.float32)) / GLA_GATE_NORMALIZER
    hk = (B, T, GLA_HEADS, GLA_DK)
    qh = q.astype(jnp.float32).reshape(hk) * (GLA_DK ** -0.5)
    kh = k.astype(jnp.float32).reshape(hk)
    vh = v.astype(jnp.float32).reshape(B, T, GLA_HEADS, GLA_DV)
    o, S = gla_recurrence(qh, kh, vh, log_a.reshape(hk), s0.astype(jnp.float32))
    o = o * lax.rsqrt(jnp.mean(o * o, axis=-1, keepdims=True) + EPS) * norm_w.astype(jnp.float32)
    o = o.reshape(B, T, GLA_VAL_DIM) * jax.nn.silu(g.astype(jnp.float32))
    return (o.astype(xn.dtype) @ w_out).astype(xn.dtype), S


def swiglu(xn, w_gate, w_up, w_down):
    return (jax.nn.silu(xn @ w_gate) * (xn @ w_up)) @ w_down


def trunk(x, p, pool_hist, gla_s0, start, norm_mix, norm_ffn, norm_ple, norm_final,
          w_pool, b_pool, pool_scale, w_gla_in, w_gla_gate_up, b_gla_gate, gla_norm, w_gla_out,
          w_ffn_gate, w_ffn_up, w_ffn_down, w_ple_proj, w_ple_gate):
    h = x
    pool_states, gla_states = [], []
    for i in range(DEPTH):
        j = i // N_MIXERS
        xn = rmsnorm(h, norm_mix[i])
        if i % N_MIXERS == 0:
            out, st = pool_mixer(xn, pool_hist[j], start, w_pool[j], b_pool[j], pool_scale[j])
            pool_states.append(st)
        else:
            out, st = gla_mixer(xn, gla_s0[j], w_gla_in[j], w_gla_gate_up[j], b_gla_gate[j],
                                gla_norm[j], w_gla_out[j])
            gla_states.append(st)
        h = h + out
        h = h + swiglu(rmsnorm(h, norm_ffn[i]), w_ffn_gate[i], w_ffn_up[i], w_ffn_down[i])
        gate = jax.nn.sigmoid(rmsnorm(h, norm_ple[i]) @ w_ple_gate[i])
        h = h + gate * (p[i].astype(h.dtype) @ w_ple_proj[i])
    return rmsnorm(h, norm_final), jnp.stack(pool_states), jnp.stack(gla_states)


def setup_inputs(seed: int = 0) -> dict:
    key = jax.random.key(seed)
    ks = jax.random.split(key, 32)
    f32 = jnp.float32

    def nrm(k, shape, scale=1.0):
        return jax.random.normal(k, shape, f32) * scale

    def gain(k, shape):
        return 1.0 + 0.05 * jax.random.normal(k, shape, f32)

    return {
        "x_prompt": nrm(ks[0], (BATCH, SEQ, D_MODEL)),
        "x_sample": nrm(ks[1], (DEC_BATCH, DEC_SEQ, D_MODEL)),
        "state_pool": nrm(ks[2], (N_POOL_LAYERS, DEC_BATCH, POOL_HIST, D_MODEL)),
        "state_gla": nrm(ks[3], (N_GLA_LAYERS, DEC_BATCH, GLA_HEADS, GLA_DK, GLA_DV), 0.5),
        "p_prompt": nrm(ks[4], (DEPTH, BATCH, SEQ, PLE_DIM)),
        "p_sample": nrm(ks[5], (DEPTH, DEC_BATCH, DEC_SEQ, PLE_DIM)),
        "norm_mix": gain(ks[6], (DEPTH, D_MODEL)),
        "norm_ffn": gain(ks[7], (DEPTH, D_MODEL)),
        "norm_ple": gain(ks[8], (DEPTH, D_MODEL)),
        "norm_final": gain(ks[9], (D_MODEL,)),
        "w_pool": nrm(ks[10], (N_POOL_LAYERS, N_POOL_GROUPS, POOL_GROUP, POOL_GROUP), POOL_GROUP ** -0.5),
        "b_pool": nrm(ks[11], (N_POOL_LAYERS, D_MODEL), 0.02),
        "pool_scale": 0.5 + 0.05 * jax.random.normal(ks[12], (N_POOL_LAYERS, D_MODEL), f32),
        "w_gla_in": nrm(ks[13], (N_GLA_LAYERS, D_MODEL, GLA_IN), D_MODEL ** -0.5),
        "w_gla_gate_up": nrm(ks[14], (N_GLA_LAYERS, GLA_GATE_RANK, GLA_KEY_DIM), GLA_GATE_RANK ** -0.5),
        "b_gla_gate": nrm(ks[15], (N_GLA_LAYERS, GLA_KEY_DIM), 0.1),
        "gla_norm": gain(ks[16], (N_GLA_LAYERS, GLA_DV)),
        "w_gla_out": nrm(ks[17], (N_GLA_LAYERS, GLA_VAL_DIM, D_MODEL), GLA_VAL_DIM ** -0.5),
        "w_ffn_gate": nrm(ks[18], (DEPTH, D_MODEL, D_FF), D_MODEL ** -0.5),
        "w_ffn_up": nrm(ks[19], (DEPTH, D_MODEL, D_FF), D_MODEL ** -0.5),
        "w_ffn_down": nrm(ks[20], (DEPTH, D_FF, D_MODEL), D_FF ** -0.5),
        "w_ple_proj": nrm(ks[21], (DEPTH, PLE_DIM, D_MODEL), PLE_DIM ** -0.5),
        "w_ple_gate": nrm(ks[22], (DEPTH, D_MODEL, D_MODEL), D_MODEL ** -0.5),
    }


def reference(x_prompt, x_sample, state_pool, state_gla, p_prompt, p_sample,
              norm_mix, norm_ffn, norm_ple, norm_final, w_pool, b_pool, pool_scale,
              w_gla_in, w_gla_gate_up, b_gla_gate, gla_norm, w_gla_out,
              w_ffn_gate, w_ffn_up, w_ffn_down, w_ple_proj, w_ple_gate):
    weights = (norm_mix, norm_ffn, norm_ple, norm_final, w_pool, b_pool, pool_scale,
               w_gla_in, w_gla_gate_up, b_gla_gate, gla_norm, w_gla_out,
               w_ffn_gate, w_ffn_up, w_ffn_down, w_ple_proj, w_ple_gate)
    b = x_prompt.shape[0]
    pool_hist0 = jnp.zeros((N_POOL_LAYERS, b, POOL_HIST, D_MODEL), x_prompt.dtype)
    gla_s00 = jnp.zeros((N_GLA_LAYERS, b, GLA_HEADS, GLA_DK, GLA_DV), jnp.float32)
    y_prompt, pool_state_prompt, gla_state_prompt = trunk(
        x_prompt, p_prompt, pool_hist0, gla_s00, 0, *weights)
    y_sample, pool_state_sample, gla_state_sample = trunk(
        x_sample, p_sample, state_pool, state_gla, PAST_LEN, *weights)
    return (y_prompt, y_sample, pool_state_prompt, pool_state_sample, gla_state_prompt, gla_state_sample)
```

```python
import functools

import jax
import jax.numpy as jnp
from jax import lax
from jax.experimental import pallas as pl
from jax.experimental.pallas import tpu as pltpu

D_MODEL = 1024
PAST_LEN = 2048
CHUNK = 64
POOL_WINDOWS = (2, 4, 8, 16)
POOL_GROUP = D_MODEL // len(POOL_WINDOWS)
POOL_HIST = max(POOL_WINDOWS) - 1
HIST_PAD = POOL_HIST + 1
GLA_HEADS = 4
GLA_KEY_DIM = D_MODEL // 2
GLA_VAL_DIM = D_MODEL
GLA_DK = GLA_KEY_DIM // GLA_HEADS
GLA_DV = GLA_VAL_DIM // GLA_HEADS
GLA_GATE_RANK = 16
GLA_GATE_NORMALIZER = 16.0
GATE_RANK_PAD = 128
D_FF = 2816
PLE_DIM = 256
EPS = 1e-6

VMEM_LIMIT_BYTES = 58 * 1024 * 1024

BF16 = jnp.bfloat16
F32 = jnp.float32

V_NORM_MIX, V_NORM_FFN, V_NORM_PLE, V_A, V_B, V_C = 0, 1, 2, 3, 4, 5
VEC_ROWS = 8


def _dot(a, b):
    return jnp.dot(a, b, preferred_element_type=F32)


def _rms(x, g):
    ms = jnp.mean(x * x, axis=-1, keepdims=True)
    return x * lax.rsqrt(ms + EPS) * g


def _ffn_ple(h, p_bf, vec_ref, wg_ref, wu_ref, wd_ref, wpg_ref, wpp_ref):
    xn = _rms(h, vec_ref[V_NORM_FFN:V_NORM_FFN + 1, :]).astype(BF16)
    gate = _dot(xn, wg_ref[...])
    up = _dot(xn, wu_ref[...])
    act = (gate * jax.nn.sigmoid(gate) * up).astype(BF16)
    h = h + _dot(act, wd_ref[...])
    xn = _rms(h, vec_ref[V_NORM_PLE:V_NORM_PLE + 1, :]).astype(BF16)
    ple_gate = jax.nn.sigmoid(_dot(xn, wpg_ref[...]))
    return h + ple_gate * _dot(p_bf, wpp_ref[...])


def _pool_layer_kernel(x_ref, p_ref, hist_ref, vec_ref, wpool_ref, wg_ref, wu_ref, wd_ref,
                       wpg_ref, wpp_ref, h_out_ref, state_out_ref, buf_ref, d_ref,
                       *, nb, tm, start):
    t = pl.program_id(1)
    m = nb * tm

    @pl.when(t == 0)
    def _():
        buf_ref[:, 0:1, :] = jnp.zeros((nb, 1, D_MODEL), F32)
        buf_ref[:, 1:HIST_PAD, :] = hist_ref[...]

    x = x_ref[...].reshape(m, D_MODEL)
    xn = _rms(x, vec_ref[V_NORM_MIX:V_NORM_MIX + 1, :])
    buf_ref[:, HIST_PAD:HIST_PAD + tm, :] = xn.reshape(nb, tm, D_MODEL)

    pos = start + t * tm + lax.broadcasted_iota(jnp.int32, (tm, 1), 0)
    for gi, w in enumerate(POOL_WINDOWS):
        sl = slice(gi * POOL_GROUP, (gi + 1) * POOL_GROUP)
        inv_cnt = 1.0 / jnp.minimum(w, pos + 1).astype(F32)
        for bi in range(nb):
            cur = buf_ref[bi, HIST_PAD:HIST_PAD + tm, sl]
            acc = cur
            for j in range(1, w):
                acc = acc + buf_ref[bi, HIST_PAD - j:HIST_PAD - j + tm, sl]
            d_ref[bi * tm:(bi + 1) * tm, sl] = (acc * inv_cnt - cur).astype(BF16)

    ys = [_dot(d_ref[:, gi * POOL_GROUP:(gi + 1) * POOL_GROUP], wpool_ref[gi])
          for gi in range(len(POOL_WINDOWS))]
    y = jnp.concatenate(ys, axis=1)
    y = (y + vec_ref[V_A:V_A + 1, :]) * vec_ref[V_B:V_B + 1, :]
    h = x + y

    p_bf = p_ref[...].reshape(m, PLE_DIM).astype(BF16)
    h = _ffn_ple(h, p_bf, vec_ref, wg_ref, wu_ref, wd_ref, wpg_ref, wpp_ref)
    h_out_ref[...] = h.reshape(nb, tm, D_MODEL)

    @pl.when(t == pl.num_programs(1) - 1)
    def _():
        state_out_ref[...] = buf_ref[:, tm + 1:tm + HIST_PAD, :]

    buf_ref[:, 0:HIST_PAD, :] = buf_ref[:, tm:tm + HIST_PAD, :]


def _const_spec(shape):
    nd = len(shape)
    return pl.BlockSpec(shape, lambda b, t: (0,) * nd, pipeline_mode=pl.Buffered(1))


def _pool_layer(x, p, layer, hist, vecs, wpool, wg, wu, wd, wpg, wpp, *, nb, tm, start):
    B, T, _ = x.shape
    grid = (B // nb, T // tm)
    kern = functools.partial(_pool_layer_kernel, nb=nb, tm=tm, start=start)
    return pl.pallas_call(
        kern,
        grid=grid,
        in_specs=[
            pl.BlockSpec((nb, tm, D_MODEL), lambda b, t: (b, t, 0)),
            pl.BlockSpec((None, nb, tm, PLE_DIM), lambda b, t: (layer, b, t, 0)),
            pl.BlockSpec((nb, POOL_HIST, D_MODEL), lambda b, t: (b, 0, 0)),
            _const_spec(vecs.shape), _const_spec(wpool.shape), _const_spec(wg.shape),
            _const_spec(wu.shape), _const_spec(wd.shape), _const_spec(wpg.shape),
            _const_spec(wpp.shape),
        ],
        out_specs=[
            pl.BlockSpec((nb, tm, D_MODEL), lambda b, t: (b, t, 0)),
            pl.BlockSpec((nb, POOL_HIST, D_MODEL), lambda b, t: (b, 0, 0)),
        ],
        out_shape=[
            jax.ShapeDtypeStruct((B, T, D_MODEL), F32),
            jax.ShapeDtypeStruct((B, POOL_HIST, D_MODEL), F32),
        ],
        scratch_shapes=[
            pltpu.VMEM((nb, HIST_PAD + tm, D_MODEL), F32),
            pltpu.VMEM((nb * tm, D_MODEL), BF16),
        ],
        compiler_params=pltpu.CompilerParams(
            dimension_semantics=("arbitrary", "arbitrary"),
            vmem_limit_bytes=VMEM_LIMIT_BYTES),
        name="pool_layer",
    )(x, p, hist, vecs, wpool, wg, wu, wd, wpg, wpp)


def _gla_layer_kernel(h_ref, p_ref, s0_ref, vec_ref, wq_ref, wk_ref, wv_ref, wgg_ref, wgr_ref,
                      wgu_ref, wo_ref, wg_ref, wu_ref, wd_ref, wpg_ref, wpp_ref,
                      y_out_ref, s_ref, q_s, k_s, v_s, la_s, gate_s, o_s,
                      *, nb, tm, chunk):
    t = pl.program_id(1)
    m = nb * tm
    chunks_per_batch = tm // chunk

    @pl.when(t == 0)
    def _():
        s_ref[...] = s0_ref[...]

    h = h_ref[...].reshape(m, D_MODEL)
    xn = _rms(h, vec_ref[V_NORM_MIX:V_NORM_MIX + 1, :]).astype(BF16)
    q_s[...] = _dot(xn, wq_ref[...]) * (GLA_DK ** -0.5)
    k_s[...] = _dot(xn, wk_ref[...])
    v_s[...] = _dot(xn, wv_ref[...]).astype(BF16)
    g = _dot(xn, wgg_ref[...])
    gate_s[...] = g * jax.nn.sigmoid(g)
    gr = _dot(xn, wgr_ref[...]).astype(BF16)
    z = _dot(gr, wgu_ref[...]) + vec_ref[V_A:V_A + 1, 0:GLA_KEY_DIM]
    log_sig = jnp.minimum(z, 0.0) - jnp.log1p(jnp.exp(-jnp.abs(z)))
    la_s[...] = log_sig * (1.0 / GLA_GATE_NORMALIZER)

    row = lax.broadcasted_iota(jnp.int32, (chunk, chunk), 0)
    col = lax.broadcasted_iota(jnp.int32, (chunk, chunk), 1)
    causal = row >= col
    tri = causal.astype(BF16)
    eye = (lax.broadcasted_iota(jnp.int32, (GLA_DK, GLA_DK), 0)
           == lax.broadcasted_iota(jnp.int32, (GLA_DK, GLA_DK), 1)).astype(F32)
    gnorm = vec_ref[V_B:V_B + 1, 0:GLA_DV]

    def chunk_body(ci, carry):
        r0 = pl.multiple_of(ci * chunk, chunk)
        bi = ci // chunks_per_batch
        rows = pl.ds(r0, chunk)
        la = la_s[rows, :]
        la_hi = la.astype(BF16)
        la_lo = (la - la_hi.astype(F32)).astype(BF16)
        b = _dot(tri, la_hi) + _dot(tri, la_lo)
        b_last = b[chunk - 1:chunk, :]
        qc = q_s[rows, :]
        kc = k_s[rows, :]
        q_dec = (qc * jnp.exp(b)).astype(BF16)
        k_inv = (kc * jnp.exp(-b)).astype(BF16)
        k_end = (kc * jnp.exp(b_last - b)).astype(BF16)
        e_last = jnp.exp(b_last)
        for hh in range(GLA_HEADS):
            ks = slice(hh * GLA_DK, (hh + 1) * GLA_DK)
            vs = slice(hh * GLA_DV, (hh + 1) * GLA_DV)
            qd_h = q_dec[:, ks]
            v_h = v_s[rows, vs]
            scores = lax.dot_general(qd_h, k_inv[:, ks], (((1,), (1,)), ((), ())),
                                     preferred_element_type=F32)
            scores = jnp.where(causal, scores, 0.0).astype(BF16)
            s_old = s_ref[bi, hh]
            o = _dot(scores, v_h) + _dot(qd_h, s_old.astype(BF16))
            e_col = jnp.sum(eye * e_last[:, ks], axis=1, keepdims=True)
            kv = lax.dot_general(k_end[:, ks], v_h, (((0,), (0,)), ((), ())),
                                 preferred_element_type=F32)
            s_ref[bi, hh] = e_col * s_old + kv
            o = o * lax.rsqrt(jnp.mean(o * o, axis=-1, keepdims=True) + EPS) * gnorm
            o_s[rows, vs] = (o * gate_s[rows, vs]).astype(BF16)
        return carry

    lax.fori_loop(0, nb * chunks_per_batch, chunk_body, 0)

    h = h + _dot(o_s[...], wo_ref[...])
    p_bf = p_ref[...].reshape(m, PLE_DIM).astype(BF16)
    h = _ffn_ple(h, p_bf, vec_ref, wg_ref, wu_ref, wd_ref, wpg_ref, wpp_ref)
    y = _rms(h, vec_ref[V_C:V_C + 1, :])
    y_out_ref[...] = y.reshape(nb, tm, D_MODEL)


def _gla_layer(h, p, layer, s0, vecs, wq, wk, wv, wgg, wgr, wgu, wo, wg, wu, wd, wpg, wpp,
               *, nb, tm):
    B, T, _ = h.shape
    chunk = min(CHUNK, T)
    grid = (B // nb, T // tm)
    m = nb * tm
    kern = functools.partial(_gla_layer_kernel, nb=nb, tm=tm, chunk=chunk)
    weights = (vecs, wq, wk, wv, wgg, wgr, wgu, wo, wg, wu, wd, wpg, wpp)
    state_spec = pl.BlockSpec((nb, GLA_HEADS, GLA_DK, GLA_DV), lambda b, t: (b, 0, 0, 0))
    return pl.pallas_call(
        kern,
        grid=grid,
        in_specs=[
            pl.BlockSpec((nb, tm, D_MODEL), lambda b, t: (b, t, 0)),
            pl.BlockSpec((None, nb, tm, PLE_DIM), lambda b, t: (layer, b, t, 0)),
            state_spec,
        ] + [_const_spec(w.shape) for w in weights],
        out_specs=[
            pl.BlockSpec((nb, tm, D_MODEL), lambda b, t: (b, t, 0)),
            state_spec,
        ],
        out_shape=[
            jax.ShapeDtypeStruct((B, T, D_MODEL), F32),
            jax.ShapeDtypeStruct((B, GLA_HEADS, GLA_DK, GLA_DV), F32),
        ],
        scratch_shapes=[
            pltpu.VMEM((m, GLA_KEY_DIM), F32),
            pltpu.VMEM((m, GLA_KEY_DIM), F32),
            pltpu.VMEM((m, GLA_VAL_DIM), BF16),
            pltpu.VMEM((m, GLA_KEY_DIM), F32),
            pltpu.VMEM((m, GLA_VAL_DIM), F32),
            pltpu.VMEM((m, GLA_VAL_DIM), BF16),
        ],
        compiler_params=pltpu.CompilerParams(
            dimension_semantics=("arbitrary", "arbitrary"),
            vmem_limit_bytes=VMEM_LIMIT_BYTES),
        name="gla_layer",
    )(h, p, s0, *weights)


def _pack_rows(rows):
    out = [jnp.pad(r.astype(F32), (0, D_MODEL - r.shape[0])) for r in rows]
    out += [jnp.zeros((D_MODEL,), F32)] * (VEC_ROWS - len(out))
    return jnp.stack(out)


def kernel(x_prompt, x_sample, state_pool, state_gla, p_prompt, p_sample, norm_mix, norm_ffn,
           norm_ple, norm_final, w_pool, b_pool, pool_scale, w_gla_in, w_gla_gate_up, b_gla_gate,
           gla_norm, w_gla_out, w_ffn_gate, w_ffn_up, w_ffn_down, w_ple_proj, w_ple_gate):
    vec0 = _pack_rows([norm_mix[0], norm_ffn[0], norm_ple[0], b_pool[0], pool_scale[0]])
    vec1 = _pack_rows([norm_mix[1], norm_ffn[1], norm_ple[1], b_gla_gate[0], gla_norm[0], norm_final])
    wpool = w_pool[0].astype(BF16)
    ffn = [(w_ffn_gate[i].astype(BF16), w_ffn_up[i].astype(BF16), w_ffn_down[i].astype(BF16),
            w_ple_gate[i].astype(BF16), w_ple_proj[i].astype(BF16)) for i in range(2)]
    w_in = w_gla_in[0].astype(BF16)
    k0, k1, k2, k3 = GLA_KEY_DIM, 2 * GLA_KEY_DIM, 2 * GLA_KEY_DIM + GLA_VAL_DIM, \
        2 * GLA_KEY_DIM + 2 * GLA_VAL_DIM
    wq, wk, wv, wgg = w_in[:, :k0], w_in[:, k0:k1], w_in[:, k1:k2], w_in[:, k2:k3]
    wgr = jnp.pad(w_in[:, k3:], ((0, 0), (0, GATE_RANK_PAD - GLA_GATE_RANK)))
    wgu = jnp.pad(w_gla_gate_up[0].astype(BF16), ((0, GATE_RANK_PAD - GLA_GATE_RANK), (0, 0)))
    wo = w_gla_out[0].astype(BF16)

    bp = x_prompt.shape[0]
    hist0 = jnp.zeros((bp, POOL_HIST, D_MODEL), F32)
    s00 = jnp.zeros((bp, GLA_HEADS, GLA_DK, GLA_DV), F32)

    def trunk(x, p, hist, s0, start, nb0, nb1, tm):
        h1, pool_state = _pool_layer(x, p, 0, hist, vec0, wpool, *ffn[0], nb=nb0, tm=tm, start=start)
        y, gla_state = _gla_layer(h1, p, 1, s0, vec1, wq, wk, wv, wgg, wgr, wgu, wo, *ffn[1],
                                  nb=nb1, tm=tm)
        return y, pool_state[None], gla_state[None]

    y_p, ps_p, gs_p = trunk(x_prompt, p_prompt, hist0, s00, 0, 1, 1, 256)
    y_s, ps_s, gs_s = trunk(x_sample, p_sample, state_pool[0], state_gla[0], PAST_LEN,
                            x_sample.shape[0], 8, x_sample.shape[1])
    return (y_p, y_s, ps_p, ps_s, gs_p, gs_s)
```

```python
import functools

import jax
import jax.numpy as jnp
from jax import lax
from jax.experimental import pallas as pl
from jax.experimental.pallas import tpu as pltpu

D_MODEL = 1024
PAST_LEN = 2048
CHUNK = 64
POOL_WINDOWS = (2, 4, 8, 16)
POOL_GROUP = D_MODEL // len(POOL_WINDOWS)
POOL_HIST = max(POOL_WINDOWS) - 1
HIST_PAD = POOL_HIST + 1
GLA_HEADS = 4
GLA_KEY_DIM = D_MODEL // 2
GLA_VAL_DIM = D_MODEL
GLA_DK = GLA_KEY_DIM // GLA_HEADS
GLA_DV = GLA_VAL_DIM // GLA_HEADS
GLA_GATE_RANK = 16
GLA_GATE_NORMALIZER = 16.0
GATE_RANK_PAD = 128
IN_Q, IN_K, IN_V = 0, GLA_KEY_DIM, 2 * GLA_KEY_DIM
IN_G = IN_V + GLA_VAL_DIM
IN_R = IN_G + GLA_VAL_DIM
IN_END = IN_R + GATE_RANK_PAD
D_FF = 2816
PLE_DIM = 256
EPS = 1e-6

VMEM_LIMIT_BYTES = 58 * 1024 * 1024

BF16 = jnp.bfloat16
F32 = jnp.float32

V_NORM_MIX, V_NORM_FFN, V_NORM_PLE, V_A, V_B, V_C = 0, 1, 2, 3, 4, 5
VEC_ROWS = 8


def _dot(a, b):
    return jnp.dot(a, b, preferred_element_type=F32)


def _rms(x, g):
    ms = jnp.mean(x * x, axis=-1, keepdims=True)
    return x * lax.rsqrt(ms + EPS) * g


def _ffn_ple(h, p_bf, vec_ref, wg_ref, wu_ref, wd_ref, wpg_ref, wpp_ref):
    xn = _rms(h, vec_ref[V_NORM_FFN:V_NORM_FFN + 1, :]).astype(BF16)
    gate = _dot(xn, wg_ref[...])
    up = _dot(xn, wu_ref[...])
    act = (gate * jax.nn.sigmoid(gate) * up).astype(BF16)
    h = h + _dot(act, wd_ref[...])
    xn = _rms(h, vec_ref[V_NORM_PLE:V_NORM_PLE + 1, :]).astype(BF16)
    ple_gate = jax.nn.sigmoid(_dot(xn, wpg_ref[...]))
    return h + ple_gate * _dot(p_bf, wpp_ref[...])


def _pool_layer_kernel(x_ref, p_ref, hist_ref, vec_ref, wpool_ref, wg_ref, wu_ref, wd_ref,
                       wpg_ref, wpp_ref, h_out_ref, state_out_ref, buf_ref, d_ref,
                       *, nb, tm, start):
    t = pl.program_id(1)
    m = nb * tm

    @pl.when(t == 0)
    def _():
        buf_ref[:, 0:1, :] = jnp.zeros((nb, 1, D_MODEL), F32)
        buf_ref[:, 1:HIST_PAD, :] = hist_ref[...]

    x = x_ref[...].reshape(m, D_MODEL)
    xn = _rms(x, vec_ref[V_NORM_MIX:V_NORM_MIX + 1, :])
    buf_ref[:, HIST_PAD:HIST_PAD + tm, :] = xn.reshape(nb, tm, D_MODEL)

    pos = start + t * tm + lax.broadcasted_iota(jnp.int32, (tm, 1), 0)
    for gi, w in enumerate(POOL_WINDOWS):
        sl = slice(gi * POOL_GROUP, (gi + 1) * POOL_GROUP)
        inv_cnt = 1.0 / jnp.minimum(w, pos + 1).astype(F32)
        for bi in range(nb):
            cur = buf_ref[bi, HIST_PAD:HIST_PAD + tm, sl]
            acc = cur
            for j in range(1, w):
                acc = acc + buf_ref[bi, HIST_PAD - j:HIST_PAD - j + tm, sl]
            d_ref[bi * tm:(bi + 1) * tm, sl] = (acc * inv_cnt - cur).astype(BF16)

    ys = [_dot(d_ref[:, gi * POOL_GROUP:(gi + 1) * POOL_GROUP], wpool_ref[gi])
          for gi in range(len(POOL_WINDOWS))]
    y = jnp.concatenate(ys, axis=1)
    y = (y + vec_ref[V_A:V_A + 1, :]) * vec_ref[V_B:V_B + 1, :]
    h = x + y

    p_bf = p_ref[...].reshape(m, PLE_DIM).astype(BF16)
    h = _ffn_ple(h, p_bf, vec_ref, wg_ref, wu_ref, wd_ref, wpg_ref, wpp_ref)
    h_out_ref[...] = h.reshape(nb, tm, D_MODEL)

    @pl.when(t == pl.num_programs(1) - 1)
    def _():
        state_out_ref[...] = buf_ref[:, tm + 1:tm + HIST_PAD, :]

    buf_ref[:, 0:HIST_PAD, :] = buf_ref[:, tm:tm + HIST_PAD, :]


def _layer_spec(shape, layer):
    nd = len(shape) - 1
    return pl.BlockSpec((None,) + tuple(shape[1:]), lambda b, t: (layer,) + (0,) * nd,
                        pipeline_mode=pl.Buffered(1))


def _pool_layer(x, p, layer, hist, vecs, wpool, wg, wu, wd, wpg, wpp, *, nb, tm, start):
    weights = (vecs, wpool, wg, wu, wd, wpg, wpp)
    wlayer = (0, 0, layer, layer, layer, layer, layer)
    B, T, _ = x.shape
    grid = (B // nb, T // tm)
    kern = functools.partial(_pool_layer_kernel, nb=nb, tm=tm, start=start)
    return pl.pallas_call(
        kern,
        grid=grid,
        in_specs=[
            pl.BlockSpec((nb, tm, D_MODEL), lambda b, t: (b, t, 0)),
            pl.BlockSpec((None, nb, tm, PLE_DIM), lambda b, t: (layer, b, t, 0)),
            pl.BlockSpec((nb, POOL_HIST, D_MODEL), lambda b, t: (b, 0, 0)),
        ] + [_layer_spec(w.shape, l) for w, l in zip(weights, wlayer)],
        out_specs=[
            pl.BlockSpec((nb, tm, D_MODEL), lambda b, t: (b, t, 0)),
            pl.BlockSpec((nb, POOL_HIST, D_MODEL), lambda b, t: (b, 0, 0)),
        ],
        out_shape=[
            jax.ShapeDtypeStruct((B, T, D_MODEL), F32),
            jax.ShapeDtypeStruct((B, POOL_HIST, D_MODEL), F32),
        ],
        scratch_shapes=[
            pltpu.VMEM((nb, HIST_PAD + tm, D_MODEL), F32),
            pltpu.VMEM((nb * tm, D_MODEL), BF16),
        ],
        compiler_params=pltpu.CompilerParams(
            dimension_semantics=("arbitrary", "arbitrary"),
            vmem_limit_bytes=VMEM_LIMIT_BYTES),
        name="pool_layer",
    )(x, p, hist, *weights)


def _gla_layer_kernel(h_ref, p_ref, s0_ref, vec_ref, win_ref, wgu_ref, wo_ref,
                      wg_ref, wu_ref, wd_ref, wpg_ref, wpp_ref,
                      y_out_ref, s_ref, o_s, *, nb, tm, chunk):
    t = pl.program_id(1)
    m = nb * tm
    chunks_per_batch = tm // chunk

    @pl.when(t == 0)
    def _():
        s_ref[...] = s0_ref[...]

    h = h_ref[...].reshape(m, D_MODEL)
    xn = _rms(h, vec_ref[V_NORM_MIX:V_NORM_MIX + 1, :]).astype(BF16)
    q = _dot(xn, win_ref[:, IN_Q:IN_K]) * (GLA_DK ** -0.5)
    k = _dot(xn, win_ref[:, IN_K:IN_V])
    v = _dot(xn, win_ref[:, IN_V:IN_G]).astype(BF16)
    g = _dot(xn, win_ref[:, IN_G:IN_R])
    out_gate = g * jax.nn.sigmoid(g)
    gr = _dot(xn, win_ref[:, IN_R:IN_END]).astype(BF16)
    z = _dot(gr, wgu_ref[...]) + vec_ref[V_A:V_A + 1, 0:GLA_KEY_DIM]
    log_sig = jnp.minimum(z, 0.0) - jnp.log1p(jnp.exp(-jnp.abs(z)))
    la = log_sig * (1.0 / GLA_GATE_NORMALIZER)

    row = lax.broadcasted_iota(jnp.int32, (m, m), 0)
    col = lax.broadcasted_iota(jnp.int32, (m, m), 1)
    tri = ((col <= row) & (col >= (row & (-chunk)))).astype(BF16)
    la_hi = la.astype(BF16)
    la_lo = (la - la_hi.astype(F32)).astype(BF16)
    b = _dot(tri, la_hi) + _dot(tri, la_lo)
    q_dec = (q * jnp.exp(b)).astype(BF16)
    k_inv = (k * jnp.exp(-b)).astype(BF16)

    causal = (lax.broadcasted_iota(jnp.int32, (chunk, chunk), 0)
              >= lax.broadcasted_iota(jnp.int32, (chunk, chunk), 1))
    eye = (lax.broadcasted_iota(jnp.int32, (GLA_DK, GLA_DK), 0)
           == lax.broadcasted_iota(jnp.int32, (GLA_DK, GLA_DK), 1)).astype(F32)
    gnorm = vec_ref[V_B:V_B + 1, 0:GLA_DV]

    for bi in range(nb):
        state = [s_ref[bi, hh] for hh in range(GLA_HEADS)]
        for ci in range(chunks_per_batch):
            r0 = bi * tm + ci * chunk
            rows = slice(r0, r0 + chunk)
            b_c = b[rows, :]
            b_last = b_c[chunk - 1:chunk, :]
            k_end = (k[rows, :] * jnp.exp(b_last - b_c)).astype(BF16)
            e_last = jnp.exp(b_last)
            for hh in range(GLA_HEADS):
                ks = slice(hh * GLA_DK, (hh + 1) * GLA_DK)
                vs = slice(hh * GLA_DV, (hh + 1) * GLA_DV)
                qd_h = q_dec[rows, ks]
                v_h = v[rows, vs]
                scores = lax.dot_general(qd_h, k_inv[rows, ks], (((1,), (1,)), ((), ())),
                                         preferred_element_type=F32)
                scores = jnp.where(causal, scores, 0.0).astype(BF16)
                o = _dot(scores, v_h) + _dot(qd_h, state[hh].astype(BF16))
                e_col = jnp.sum(eye * e_last[:, ks], axis=1, keepdims=True)
                kv = lax.dot_general(k_end[:, ks], v_h, (((0,), (0,)), ((), ())),
                                     preferred_element_type=F32)
                state[hh] = e_col * state[hh] + kv
                o = o * lax.rsqrt(jnp.mean(o * o, axis=-1, keepdims=True) + EPS) * gnorm
                o_s[rows, vs] = (o * out_gate[rows, vs]).astype(BF16)
        for hh in range(GLA_HEADS):
            s_ref[bi, hh] = state[hh]

    h = h + _dot(o_s[...], wo_ref[...])
    p_bf = p_ref[...].reshape(m, PLE_DIM).astype(BF16)
    h = _ffn_ple(h, p_bf, vec_ref, wg_ref, wu_ref, wd_ref, wpg_ref, wpp_ref)
    y = _rms(h, vec_ref[V_C:V_C + 1, :])
    y_out_ref[...] = y.reshape(nb, tm, D_MODEL)


def _gla_layer(h, p, layer, s0, vecs, win, wgu, wo, wg, wu, wd, wpg, wpp, *, nb, tm):
    B, T, _ = h.shape
    chunk = min(CHUNK, T)
    grid = (B // nb, T // tm)
    m = nb * tm
    kern = functools.partial(_gla_layer_kernel, nb=nb, tm=tm, chunk=chunk)
    weights = (vecs, win, wgu, wo, wg, wu, wd, wpg, wpp)
    wlayer = (0, 0, 0, 0, layer, layer, layer, layer, layer)
    state_spec = pl.BlockSpec((nb, GLA_HEADS, GLA_DK, GLA_DV), lambda b, t: (b, 0, 0, 0))
    return pl.pallas_call(
        kern,
        grid=grid,
        in_specs=[
            pl.BlockSpec((nb, tm, D_MODEL), lambda b, t: (b, t, 0)),
            pl.BlockSpec((None, nb, tm, PLE_DIM), lambda b, t: (layer, b, t, 0)),
            state_spec,
        ] + [_layer_spec(w.shape, l) for w, l in zip(weights, wlayer)],
        out_specs=[
            pl.BlockSpec((nb, tm, D_MODEL), lambda b, t: (b, t, 0)),
            state_spec,
        ],
        out_shape=[
            jax.ShapeDtypeStruct((B, T, D_MODEL), F32),
            jax.ShapeDtypeStruct((B, GLA_HEADS, GLA_DK, GLA_DV), F32),
        ],
        scratch_shapes=[
            pltpu.VMEM((m, GLA_VAL_DIM), BF16),
        ],
        compiler_params=pltpu.CompilerParams(
            dimension_semantics=("arbitrary", "arbitrary"),
            vmem_limit_bytes=VMEM_LIMIT_BYTES),
        name="gla_layer",
    )(h, p, s0, *weights)


def _pack_rows(rows):
    out = [jnp.pad(r.astype(F32), (0, D_MODEL - r.shape[0])) for r in rows]
    out += [jnp.zeros((D_MODEL,), F32)] * (VEC_ROWS - len(out))
    return jnp.stack(out)


def kernel(x_prompt, x_sample, state_pool, state_gla, p_prompt, p_sample, norm_mix, norm_ffn,
           norm_ple, norm_final, w_pool, b_pool, pool_scale, w_gla_in, w_gla_gate_up, b_gla_gate,
           gla_norm, w_gla_out, w_ffn_gate, w_ffn_up, w_ffn_down, w_ple_proj, w_ple_gate):
    vec0 = _pack_rows([norm_mix[0], norm_ffn[0], norm_ple[0], b_pool[0], pool_scale[0]])[None]
    vec1 = _pack_rows([norm_mix[1], norm_ffn[1], norm_ple[1], b_gla_gate[0], gla_norm[0],
                       norm_final])[None]
    wpool = w_pool.astype(BF16)
    ffn = (w_ffn_gate.astype(BF16), w_ffn_up.astype(BF16), w_ffn_down.astype(BF16),
           w_ple_gate.astype(BF16), w_ple_proj.astype(BF16))
    rank_pad = GATE_RANK_PAD - GLA_GATE_RANK
    win = jnp.pad(w_gla_in.astype(BF16), ((0, 0), (0, 0), (0, rank_pad)))
    wgu = jnp.pad(w_gla_gate_up.astype(BF16), ((0, 0), (0, rank_pad), (0, 0)))
    wo = w_gla_out.astype(BF16)

    bp = x_prompt.shape[0]
    hist0 = jnp.zeros((bp, POOL_HIST, D_MODEL), F32)
    s00 = jnp.zeros((bp, GLA_HEADS, GLA_DK, GLA_DV), F32)

    def trunk(x, p, hist, s0, start, nb0, nb1, tm):
        h1, pool_state = _pool_layer(x, p, 0, hist, vec0, wpool, *ffn, nb=nb0, tm=tm, start=start)
        y, gla_state = _gla_layer(h1, p, 1, s0, vec1, win, wgu, wo, *ffn, nb=nb1, tm=tm)
        return y, pool_state[None], gla_state[None]

    y_p, ps_p, gs_p = trunk(x_prompt, p_prompt, hist0, s00, 0, 1, 1, 256)
    y_s, ps_s, gs_s = trunk(x_sample, p_sample, state_pool[0], state_gla[0], PAST_LEN,
                            x_sample.shape[0], 8, x_sample.shape[1])
    return (y_p, y_s, ps_p, ps_s, gs_p, gs_s)
```

```python
import functools

import jax
import jax.numpy as jnp
from jax import lax
from jax.experimental import pallas as pl
from jax.experimental.pallas import tpu as pltpu

D_MODEL = 1024
PAST_LEN = 2048
CHUNK = 64
POOL_WINDOWS = (2, 4, 8, 16)
POOL_GROUP = D_MODEL // len(POOL_WINDOWS)
POOL_HIST = max(POOL_WINDOWS) - 1
HIST_PAD = POOL_HIST + 1
GLA_HEADS = 4
GLA_KEY_DIM = D_MODEL // 2
GLA_VAL_DIM = D_MODEL
GLA_DK = GLA_KEY_DIM // GLA_HEADS
GLA_DV = GLA_VAL_DIM // GLA_HEADS
GLA_GATE_RANK = 16
GLA_GATE_NORMALIZER = 16.0
GATE_RANK_PAD = 128
IN_Q, IN_K, IN_V = 0, GLA_KEY_DIM, 2 * GLA_KEY_DIM
IN_G = IN_V + GLA_VAL_DIM
IN_R = IN_G + GLA_VAL_DIM
IN_END = IN_R + GATE_RANK_PAD
D_FF = 2816
PLE_DIM = 256
EPS = 1e-6

VMEM_LIMIT_BYTES = 58 * 1024 * 1024
PROMPT_SUB = 256

BF16 = jnp.bfloat16
F32 = jnp.float32

V_NORM_MIX, V_NORM_FFN, V_NORM_PLE, V_A, V_B, V_C = 0, 1, 2, 3, 4, 5
VEC_ROWS = 8


def _dot(a, b):
    return jnp.dot(a, b, preferred_element_type=F32)


def _rms(x, g):
    ms = jnp.mean(x * x, axis=-1, keepdims=True)
    return x * lax.rsqrt(ms + EPS) * g


def _ffn_ple(h, p_bf, vec_ref, wg_ref, wu_ref, wd_ref, wpg_ref, wpp_ref):
    xn = _rms(h, vec_ref[V_NORM_FFN:V_NORM_FFN + 1, :]).astype(BF16)
    gate = _dot(xn, wg_ref[...])
    up = _dot(xn, wu_ref[...])
    act = (gate * jax.nn.sigmoid(gate) * up).astype(BF16)
    h = h + _dot(act, wd_ref[...])
    xn = _rms(h, vec_ref[V_NORM_PLE:V_NORM_PLE + 1, :]).astype(BF16)
    ple_gate = jax.nn.sigmoid(_dot(xn, wpg_ref[...]))
    return h + ple_gate * _dot(p_bf, wpp_ref[...])


def _pool_layer_kernel(x_ref, p_ref, hist_ref, vec_ref, wpool_ref, wg_ref, wu_ref, wd_ref,
                       wpg_ref, wpp_ref, h_out_ref, state_out_ref, buf_ref, d_ref,
                       *, nb, tm, sub, start):
    t = pl.program_id(1)
    ms = nb * sub

    @pl.when(t == 0)
    def _():
        buf_ref[:, 0:1, :] = jnp.zeros((nb, 1, D_MODEL), F32)
        buf_ref[:, 1:HIST_PAD, :] = hist_ref[...]

    for r0 in range(0, tm, sub):
        x = x_ref[:, r0:r0 + sub, :].reshape(ms, D_MODEL)
        xn = _rms(x, vec_ref[V_NORM_MIX:V_NORM_MIX + 1, :])
        buf_ref[:, HIST_PAD + r0:HIST_PAD + r0 + sub, :] = xn.reshape(nb, sub, D_MODEL)

        pos = start + t * tm + r0 + lax.broadcasted_iota(jnp.int32, (sub, 1), 0)
        d0 = r0 * nb
        for gi, w in enumerate(POOL_WINDOWS):
            sl = slice(gi * POOL_GROUP, (gi + 1) * POOL_GROUP)
            inv_cnt = 1.0 / jnp.minimum(w, pos + 1).astype(F32)
            for bi in range(nb):
                ext = buf_ref[bi, r0:r0 + HIST_PAD + sub, sl]
                acc = ext
                span = 1
                while span < w:
                    acc = acc + pltpu.roll(acc, span, 0)
                    span *= 2
                cur = ext[HIST_PAD:, :]
                d = acc[HIST_PAD:, :] * inv_cnt - cur
                d_ref[d0 + bi * sub:d0 + (bi + 1) * sub, sl] = d.astype(BF16)

        ys = [_dot(d_ref[d0:d0 + ms, gi * POOL_GROUP:(gi + 1) * POOL_GROUP], wpool_ref[gi])
              for gi in range(len(POOL_WINDOWS))]
        y = jnp.concatenate(ys, axis=1)
        y = (y + vec_ref[V_A:V_A + 1, :]) * vec_ref[V_B:V_B + 1, :]
        h = x + y

        p_bf = p_ref[:, r0:r0 + sub, :].reshape(ms, PLE_DIM).astype(BF16)
        h = _ffn_ple(h, p_bf, vec_ref, wg_ref, wu_ref, wd_ref, wpg_ref, wpp_ref)
        h_out_ref[:, r0:r0 + sub, :] = h.reshape(nb, sub, D_MODEL)

    @pl.when(t == pl.num_programs(1) - 1)
    def _():
        state_out_ref[...] = buf_ref[:, tm + 1:tm + HIST_PAD, :]

    buf_ref[:, 0:HIST_PAD, :] = buf_ref[:, tm:tm + HIST_PAD, :]


def _layer_spec(shape, layer):
    nd = len(shape) - 1
    return pl.BlockSpec((None,) + tuple(shape[1:]), lambda b, t: (layer,) + (0,) * nd,
                        pipeline_mode=pl.Buffered(1))


def _pool_layer(x, p, layer, hist, vecs, wpool, wg, wu, wd, wpg, wpp, *, nb, tm, sub, start):
    weights = (vecs, wpool, wg, wu, wd, wpg, wpp)
    wlayer = (0, 0, layer, layer, layer, layer, layer)
    B, T, _ = x.shape
    grid = (B // nb, T // tm)
    kern = functools.partial(_pool_layer_kernel, nb=nb, tm=tm, sub=sub, start=start)
    return pl.pallas_call(
        kern,
        grid=grid,
        in_specs=[
            pl.BlockSpec((nb, tm, D_MODEL), lambda b, t: (b, t, 0)),
            pl.BlockSpec((None, nb, tm, PLE_DIM), lambda b, t: (layer, b, t, 0)),
            pl.BlockSpec((nb, POOL_HIST, D_MODEL), lambda b, t: (b, 0, 0)),
        ] + [_layer_spec(w.shape, l) for w, l in zip(weights, wlayer)],
        out_specs=[
            pl.BlockSpec((nb, tm, D_MODEL), lambda b, t: (b, t, 0)),
            pl.BlockSpec((nb, POOL_HIST, D_MODEL), lambda b, t: (b, 0, 0)),
        ],
        out_shape=[
            jax.ShapeDtypeStruct((B, T, D_MODEL), F32),
            jax.ShapeDtypeStruct((B, POOL_HIST, D_MODEL), F32),
        ],
        scratch_shapes=[
            pltpu.VMEM((nb, HIST_PAD + tm, D_MODEL), F32),
            pltpu.VMEM((nb * tm, D_MODEL), BF16),
        ],
        compiler_params=pltpu.CompilerParams(
            dimension_semantics=("arbitrary", "arbitrary"),
            vmem_limit_bytes=VMEM_LIMIT_BYTES),
        name="pool_layer",
    )(x, p, hist, *weights)


def _gla_layer_kernel(h_ref, p_ref, s0_ref, vec_ref, win_ref, wgu_ref, wo_ref,
                      wg_ref, wu_ref, wd_ref, wpg_ref, wpp_ref,
                      y_out_ref, s_ref, o_s, *, nb, tm, chunk):
    t = pl.program_id(1)
    m = nb * tm
    chunks_per_batch = tm // chunk

    @pl.when(t == 0)
    def _():
        s_ref[...] = s0_ref[...]

    h = h_ref[...].reshape(m, D_MODEL)
    xn = _rms(h, vec_ref[V_NORM_MIX:V_NORM_MIX + 1, :]).astype(BF16)
    q = _dot(xn, win_ref[:, IN_Q:IN_K]) * (GLA_DK ** -0.5)
    k = _dot(xn, win_ref[:, IN_K:IN_V])
    v = _dot(xn, win_ref[:, IN_V:IN_G]).astype(BF16)
    g = _dot(xn, win_ref[:, IN_G:IN_R])
    out_gate = g * jax.nn.sigmoid(g)
    gr = _dot(xn, win_ref[:, IN_R:IN_END]).astype(BF16)
    z = _dot(gr, wgu_ref[...]) + vec_ref[V_A:V_A + 1, 0:GLA_KEY_DIM]
    log_sig = jnp.minimum(z, 0.0) - jnp.log1p(jnp.exp(-jnp.abs(z)))
    la = log_sig * (1.0 / GLA_GATE_NORMALIZER)

    row = lax.broadcasted_iota(jnp.int32, (m, m), 0)
    col = lax.broadcasted_iota(jnp.int32, (m, m), 1)
    tri = ((col <= row) & (col >= (row & (-chunk)))).astype(BF16)
    la_hi = la.astype(BF16)
    la_lo = (la - la_hi.astype(F32)).astype(BF16)
    b = _dot(tri, la_hi) + _dot(tri, la_lo)
    q_dec = (q * jnp.exp(b)).astype(BF16)
    k_inv = (k * jnp.exp(-b)).astype(BF16)

    causal = (lax.broadcasted_iota(jnp.int32, (chunk, chunk), 0)
              >= lax.broadcasted_iota(jnp.int32, (chunk, chunk), 1))
    eye = (lax.broadcasted_iota(jnp.int32, (GLA_DK, GLA_DK), 0)
           == lax.broadcasted_iota(jnp.int32, (GLA_DK, GLA_DK), 1)).astype(F32)
    gnorm = vec_ref[V_B:V_B + 1, 0:GLA_DV]

    for bi in range(nb):
        state = [s_ref[bi, hh] for hh in range(GLA_HEADS)]
        for ci in range(chunks_per_batch):
            r0 = bi * tm + ci * chunk
            rows = slice(r0, r0 + chunk)
            b_c = b[rows, :]
            b_last = b_c[chunk - 1:chunk, :]
            k_end = (k[rows, :] * jnp.exp(b_last - b_c)).astype(BF16)
            e_last = jnp.exp(b_last)
            for hh in range(GLA_HEADS):
                ks = slice(hh * GLA_DK, (hh + 1) * GLA_DK)
                vs = slice(hh * GLA_DV, (hh + 1) * GLA_DV)
                qd_h = q_dec[rows, ks]
                v_h = v[rows, vs]
                scores = lax.dot_general(qd_h, k_inv[rows, ks], (((1,), (1,)), ((), ())),
                                         preferred_element_type=F32)
                scores = jnp.where(causal, scores, 0.0).astype(BF16)
                o = _dot(scores, v_h) + _dot(qd_h, state[hh].astype(BF16))
                e_col = jnp.sum(eye * e_last[:, ks], axis=1, keepdims=True)
                kv = lax.dot_general(k_end[:, ks], v_h, (((0,), (0,)), ((), ())),
                                     preferred_element_type=F32)
                state[hh] = e_col * state[hh] + kv
                o = o * lax.rsqrt(jnp.mean(o * o, axis=-1, keepdims=True) + EPS) * gnorm
                o_s[rows, vs] = (o * out_gate[rows, vs]).astype(BF16)
        for hh in range(GLA_HEADS):
            s_ref[bi, hh] = state[hh]

    h = h + _dot(o_s[...], wo_ref[...])
    p_bf = p_ref[...].reshape(m, PLE_DIM).astype(BF16)
    h = _ffn_ple(h, p_bf, vec_ref, wg_ref, wu_ref, wd_ref, wpg_ref, wpp_ref)
    y = _rms(h, vec_ref[V_C:V_C + 1, :])
    y_out_ref[...] = y.reshape(nb, tm, D_MODEL)


def _gla_layer(h, p, layer, s0, vecs, win, wgu, wo, wg, wu, wd, wpg, wpp, *, nb, tm):
    B, T, _ = h.shape
    chunk = min(CHUNK, T)
    grid = (B // nb, T // tm)
    m = nb * tm
    kern = functools.partial(_gla_layer_kernel, nb=nb, tm=tm, chunk=chunk)
    weights = (vecs, win, wgu, wo, wg, wu, wd, wpg, wpp)
    wlayer = (0, 0, 0, 0, layer, layer, layer, layer, layer)
    state_spec = pl.BlockSpec((nb, GLA_HEADS, GLA_DK, GLA_DV), lambda b, t: (b, 0, 0, 0))
    return pl.pallas_call(
        kern,
        grid=grid,
        in_specs=[
            pl.BlockSpec((nb, tm, D_MODEL), lambda b, t: (b, t, 0)),
            pl.BlockSpec((None, nb, tm, PLE_DIM), lambda b, t: (layer, b, t, 0)),
            state_spec,
        ] + [_layer_spec(w.shape, l) for w, l in zip(weights, wlayer)],
        out_specs=[
            pl.BlockSpec((nb, tm, D_MODEL), lambda b, t: (b, t, 0)),
            state_spec,
        ],
        out_shape=[
            jax.ShapeDtypeStruct((B, T, D_MODEL), F32),
            jax.ShapeDtypeStruct((B, GLA_HEADS, GLA_DK, GLA_DV), F32),
        ],
        scratch_shapes=[
            pltpu.VMEM((m, GLA_VAL_DIM), BF16),
        ],
        compiler_params=pltpu.CompilerParams(
            dimension_semantics=("arbitrary", "arbitrary"),
            vmem_limit_bytes=VMEM_LIMIT_BYTES),
        name="gla_layer",
    )(h, p, s0, *weights)


def _pack_rows(rows):
    out = [jnp.pad(r.astype(F32), (0, D_MODEL - r.shape[0])) for r in rows]
    out += [jnp.zeros((D_MODEL,), F32)] * (VEC_ROWS - len(out))
    return jnp.stack(out)


def kernel(x_prompt, x_sample, state_pool, state_gla, p_prompt, p_sample, norm_mix, norm_ffn,
           norm_ple, norm_final, w_pool, b_pool, pool_scale, w_gla_in, w_gla_gate_up, b_gla_gate,
           gla_norm, w_gla_out, w_ffn_gate, w_ffn_up, w_ffn_down, w_ple_proj, w_ple_gate):
    vec0 = _pack_rows([norm_mix[0], norm_ffn[0], norm_ple[0], b_pool[0], pool_scale[0]])[None]
    vec1 = _pack_rows([norm_mix[1], norm_ffn[1], norm_ple[1], b_gla_gate[0], gla_norm[0],
                       norm_final])[None]
    wpool = w_pool.astype(BF16)
    ffn = (w_ffn_gate.astype(BF16), w_ffn_up.astype(BF16), w_ffn_down.astype(BF16),
           w_ple_gate.astype(BF16), w_ple_proj.astype(BF16))
    rank_pad = GATE_RANK_PAD - GLA_GATE_RANK
    win = jnp.pad(w_gla_in.astype(BF16), ((0, 0), (0, 0), (0, rank_pad)))
    wgu = jnp.pad(w_gla_gate_up.astype(BF16), ((0, 0), (0, rank_pad), (0, 0)))
    wo = w_gla_out.astype(BF16)

    bp = x_prompt.shape[0]
    hist0 = jnp.zeros((bp, POOL_HIST, D_MODEL), F32)
    s00 = jnp.zeros((bp, GLA_HEADS, GLA_DK, GLA_DV), F32)

    def trunk(x, p, hist, s0, start, tile0, tile1):
        h1, pool_state = _pool_layer(x, p, 0, hist, vec0, wpool, *ffn, start=start, **tile0)
        y, gla_state = _gla_layer(h1, p, 1, s0, vec1, win, wgu, wo, *ffn, **tile1)
        return y, pool_state[None], gla_state[None]

    y_p, ps_p, gs_p = trunk(x_prompt, p_prompt, hist0, s00, 0,
                            dict(nb=1, tm=2 * PROMPT_SUB, sub=PROMPT_SUB), dict(nb=1, tm=PROMPT_SUB))
    bs, ts = x_sample.shape[:2]
    y_s, ps_s, gs_s = trunk(x_sample, p_sample, state_pool[0], state_gla[0], PAST_LEN,
                            dict(nb=bs, tm=ts, sub=ts), dict(nb=bs // 2, tm=ts))
    return (y_p, y_s, ps_p, ps_s, gs_p, gs_s)
```

```python
import functools

import jax
import jax.numpy as jnp
from jax import lax
from jax.experimental import pallas as pl
from jax.experimental.pallas import tpu as pltpu

D_MODEL = 1024
PAST_LEN = 2048
CHUNK = 64
POOL_WINDOWS = (2, 4, 8, 16)
POOL_GROUP = D_MODEL // len(POOL_WINDOWS)
POOL_HIST = max(POOL_WINDOWS) - 1
HIST_PAD = POOL_HIST + 1
GLA_HEADS = 4
GLA_KEY_DIM = D_MODEL // 2
GLA_VAL_DIM = D_MODEL
GLA_DK = GLA_KEY_DIM // GLA_HEADS
GLA_DV = GLA_VAL_DIM // GLA_HEADS
GLA_GATE_RANK = 16
GLA_GATE_NORMALIZER = 16.0
GATE_RANK_PAD = 128
IN_Q, IN_K, IN_V = 0, GLA_KEY_DIM, 2 * GLA_KEY_DIM
IN_G = IN_V + GLA_VAL_DIM
IN_R = IN_G + GLA_VAL_DIM
IN_END = IN_R + GATE_RANK_PAD
D_FF = 2816
PLE_DIM = 256
EPS = 1e-6

VMEM_LIMIT_BYTES = 58 * 1024 * 1024
FFN_CHUNK = 256
PROMPT_SUB = 256
PROMPT_TILE = 2 * PROMPT_SUB

BF16 = jnp.bfloat16
F32 = jnp.float32

V_NORM_MIX, V_NORM_FFN, V_NORM_PLE, V_A, V_B, V_C = 0, 1, 2, 3, 4, 5
VEC_ROWS = 8


def _dot(a, b):
    return jnp.dot(a, b, preferred_element_type=F32)


def _rms(x, g):
    ms = jnp.mean(x * x, axis=-1, keepdims=True)
    return x * lax.rsqrt(ms + EPS) * g


def _run(gen):
    for _ in gen:
        pass


def _interleave(a, b):
    total_a, total_b = next(a), next(b)
    done_a = done_b = 0.0
    live_a = live_b = True
    while live_a or live_b:
        pick_a = live_a and (not live_b or done_a / total_a <= done_b / total_b)
        try:
            if pick_a:
                done_a += next(a)
            else:
                done_b += next(b)
        except StopIteration:
            if pick_a:
                live_a = False
            else:
                live_b = False


def _pipeline(fronts, backs):
    _run(fronts[0])
    for i in range(1, len(fronts)):
        _interleave(backs[i - 1], fronts[i])
    _run(backs[-1])


def _ffn_back(residual, p_ref, vec_ref, wg_ref, wu_ref, wd_ref, wpg_ref, wpp_ref, out_ref,
              *, nb, r0, sub, final_norm):
    ms = nb * sub
    yield 11950.0
    h = residual()
    yield 1024.0
    xn = _rms(h, vec_ref[V_NORM_FFN:V_NORM_FFN + 1, :]).astype(BF16)
    yield 400.0
    acc = None
    cols = list(range(0, D_FF, FFN_CHUNK))
    nxt = (_dot(xn, wg_ref[:, 0:FFN_CHUNK]), _dot(xn, wu_ref[:, 0:FFN_CHUNK]))
    for i, c0 in enumerate(cols):
        gate, up = nxt
        if i + 1 < len(cols):
            c1 = cols[i + 1]
            nxt = (_dot(xn, wg_ref[:, c1:c1 + FFN_CHUNK]), _dot(xn, wu_ref[:, c1:c1 + FFN_CHUNK]))
        act = (gate * jax.nn.sigmoid(gate) * up).astype(BF16)
        part = _dot(act, wd_ref[c0:c0 + FFN_CHUNK, :])
        acc = part if acc is None else acc + part
        yield 768.0
    h = h + acc
    xn = _rms(h, vec_ref[V_NORM_PLE:V_NORM_PLE + 1, :]).astype(BF16)
    yield 400.0
    ple_gate = jax.nn.sigmoid(_dot(xn, wpg_ref[...]))
    yield 1024.0
    p_bf = p_ref[:, r0:r0 + sub, :].reshape(ms, PLE_DIM).astype(BF16)
    h = h + ple_gate * _dot(p_bf, wpp_ref[...])
    yield 256.0
    if final_norm:
        h = _rms(h, vec_ref[V_C:V_C + 1, :])
    out_ref[:, r0:r0 + sub, :] = h.reshape(nb, sub, D_MODEL)
    yield 400.0


def _pool_front(x_ref, vec_ref, wpool_ref, buf_ref, d_ref, carry, *, nb, r0, sub, pos0):
    ms = nb * sub
    yield 2400.0
    x = x_ref[:, r0:r0 + sub, :].reshape(ms, D_MODEL)
    xn = _rms(x, vec_ref[V_NORM_MIX:V_NORM_MIX + 1, :])
    buf_ref[:, HIST_PAD + r0:HIST_PAD + r0 + sub, :] = xn.reshape(nb, sub, D_MODEL)
    yield 600.0
    pos = pos0 + r0 + lax.broadcasted_iota(jnp.int32, (sub, 1), 0)
    d0 = r0 * nb
    ys = []
    for gi, w in enumerate(POOL_WINDOWS):
        sl = slice(gi * POOL_GROUP, (gi + 1) * POOL_GROUP)
        inv_cnt = 1.0 / jnp.minimum(w, pos + 1).astype(F32)
        for bi in range(nb):
            ext = buf_ref[bi, r0:r0 + HIST_PAD + sub, sl]
            acc = ext
            span = 1
            while span < w:
                acc = acc + pltpu.roll(acc, span, 0)
                span *= 2
            cur = ext[HIST_PAD:, :]
            d = acc[HIST_PAD:, :] * inv_cnt - cur
            d_ref[d0 + bi * sub:d0 + (bi + 1) * sub, sl] = d.astype(BF16)
        ys.append(_dot(d_ref[d0:d0 + ms, sl], wpool_ref[gi]))
        yield 100.0 * (gi + 2)
    y = jnp.concatenate(ys, axis=1)
    y = (y + vec_ref[V_A:V_A + 1, :]) * vec_ref[V_B:V_B + 1, :]
    carry[r0] = x + y
    yield 400.0


def _pool_layer_kernel(x_ref, p_ref, hist_ref, vec_ref, wpool_ref, wg_ref, wu_ref, wd_ref,
                       wpg_ref, wpp_ref, h_out_ref, state_out_ref, buf_ref, d_ref,
                       *, nb, tm, sub, start):
    t = pl.program_id(1)

    @pl.when(t == 0)
    def _():
        buf_ref[:, 0:1, :] = jnp.zeros((nb, 1, D_MODEL), F32)
        buf_ref[:, 1:HIST_PAD, :] = hist_ref[...]

    carry = {}
    starts = list(range(0, tm, sub))
    fronts = [_pool_front(x_ref, vec_ref, wpool_ref, buf_ref, d_ref, carry,
                          nb=nb, r0=r0, sub=sub, pos0=start + t * tm) for r0 in starts]
    backs = [_ffn_back(functools.partial(carry.get, r0), p_ref, vec_ref, wg_ref, wu_ref, wd_ref,
                       wpg_ref, wpp_ref, h_out_ref, nb=nb, r0=r0, sub=sub, final_norm=False)
             for r0 in starts]
    _pipeline(fronts, backs)

    @pl.when(t == pl.num_programs(1) - 1)
    def _():
        state_out_ref[...] = buf_ref[:, tm + 1:tm + HIST_PAD, :]

    buf_ref[:, 0:HIST_PAD, :] = buf_ref[:, tm:tm + HIST_PAD, :]


def _layer_spec(shape, layer):
    nd = len(shape) - 1
    return pl.BlockSpec((None,) + tuple(shape[1:]), lambda b, t: (layer,) + (0,) * nd,
                        pipeline_mode=pl.Buffered(1))


def _pool_layer(x, p, layer, hist, vecs, wpool, wg, wu, wd, wpg, wpp, *, nb, tm, sub, start):
    weights = (vecs, wpool, wg, wu, wd, wpg, wpp)
    wlayer = (0, 0, layer, layer, layer, layer, layer)
    B, T, _ = x.shape
    grid = (B // nb, T // tm)
    kern = functools.partial(_pool_layer_kernel, nb=nb, tm=tm, sub=sub, start=start)
    return pl.pallas_call(
        kern,
        grid=grid,
        in_specs=[
            pl.BlockSpec((nb, tm, D_MODEL), lambda b, t: (b, t, 0)),
            pl.BlockSpec((None, nb, tm, PLE_DIM), lambda b, t: (layer, b, t, 0)),
            pl.BlockSpec((nb, POOL_HIST, D_MODEL), lambda b, t: (b, 0, 0)),
        ] + [_layer_spec(w.shape, l) for w, l in zip(weights, wlayer)],
        out_specs=[
            pl.BlockSpec((nb, tm, D_MODEL), lambda b, t: (b, t, 0)),
            pl.BlockSpec((nb, POOL_HIST, D_MODEL), lambda b, t: (b, 0, 0)),
        ],
        out_shape=[
            jax.ShapeDtypeStruct((B, T, D_MODEL), F32),
            jax.ShapeDtypeStruct((B, POOL_HIST, D_MODEL), F32),
        ],
        scratch_shapes=[
            pltpu.VMEM((nb, HIST_PAD + tm, D_MODEL), F32),
            pltpu.VMEM((nb * tm, D_MODEL), BF16),
        ],
        compiler_params=pltpu.CompilerParams(
            dimension_semantics=("arbitrary", "arbitrary"),
            vmem_limit_bytes=VMEM_LIMIT_BYTES),
        name="pool_layer",
    )(x, p, hist, *weights)


def _gla_front(h_ref, vec_ref, win_ref, wgu_ref, s_ref, o_s, state, *, nb, r0, sub, chunk):
    ms = nb * sub
    segs_per_batch = sub // chunk
    nseg = nb * segs_per_batch
    half = GLA_VAL_DIM // 2
    yield 9700.0
    h = h_ref[:, r0:r0 + sub, :].reshape(ms, D_MODEL)
    xn = _rms(h, vec_ref[V_NORM_MIX:V_NORM_MIX + 1, :]).astype(BF16)
    yield 400.0
    gr = _dot(xn, win_ref[:, IN_R:IN_END]).astype(BF16)
    q = _dot(xn, win_ref[:, IN_Q:IN_K]) * (GLA_DK ** -0.5)
    yield 512.0
    z = _dot(gr, wgu_ref[...]) + vec_ref[V_A:V_A + 1, 0:GLA_KEY_DIM]
    log_sig = jnp.minimum(z, 0.0) - jnp.log1p(jnp.exp(-jnp.abs(z)))
    la = log_sig * (1.0 / GLA_GATE_NORMALIZER)
    la_hi = la.astype(BF16)
    la_lo = (la - la_hi.astype(F32)).astype(BF16)
    k = _dot(xn, win_ref[:, IN_K:IN_V])
    yield 512.0
    row = lax.broadcasted_iota(jnp.int32, (ms, ms), 0)
    col = lax.broadcasted_iota(jnp.int32, (ms, ms), 1)
    causal = ((row & (-chunk)) == (col & (-chunk))) & (col <= row)
    tri = causal.astype(BF16)
    b = _dot(tri, la_hi) + _dot(tri, la_lo)
    v_parts = []
    for c0 in (IN_V, IN_V + half):
        v_parts.append(_dot(xn, win_ref[:, c0:c0 + half]).astype(BF16))
        yield 512.0
    v = jnp.concatenate(v_parts, axis=1)
    b_last = [b[(s + 1) * chunk - 1:(s + 1) * chunk, :] for s in range(nseg)]
    b_last_rows = jnp.concatenate(
        [jnp.broadcast_to(bl, (chunk, GLA_KEY_DIM)) for bl in b_last], axis=0)
    q_dec = (q * jnp.exp(b)).astype(BF16)
    k_inv = (k * jnp.exp(-b)).astype(BF16)
    k_end = (k * jnp.exp(b_last_rows - b)).astype(BF16)
    e_last = [jnp.exp(bl) for bl in b_last]
    yield 300.0
    g_parts = []
    for c0 in (IN_G, IN_G + half):
        g = _dot(xn, win_ref[:, c0:c0 + half])
        g_parts.append(g * jax.nn.sigmoid(g))
        yield 512.0
    out_gate = jnp.concatenate(g_parts, axis=1)

    eye = (lax.broadcasted_iota(jnp.int32, (GLA_DK, GLA_DK), 0)
           == lax.broadcasted_iota(jnp.int32, (GLA_DK, GLA_DK), 1)).astype(F32)
    gnorm = vec_ref[V_B:V_B + 1, 0:GLA_DV]
    row_seg = lax.broadcasted_iota(jnp.int32, (ms, GLA_DK), 0) & (-chunk)

    for hh in range(GLA_HEADS):
        ks = slice(hh * GLA_DK, (hh + 1) * GLA_DK)
        vs = slice(hh * GLA_DV, (hh + 1) * GLA_DV)
        qd_h = q_dec[:, ks]
        v_h = v[:, vs]
        scores = lax.dot_general(qd_h, k_inv[:, ks], (((1,), (1,)), ((), ())),
                                 preferred_element_type=F32)
        scores = jnp.where(causal, scores, 0.0).astype(BF16)
        o_intra = _dot(scores, v_h)
        ke_h = k_end[:, ks]
        ke_blk = jnp.concatenate(
            [jnp.where(row_seg == s * chunk, ke_h, jnp.zeros_like(ke_h)) for s in range(nseg)],
            axis=1)
        kv_all = lax.dot_general(ke_blk, v_h, (((0,), (0,)), ((), ())), preferred_element_type=F32)
        yield 700.0
        o_inter = []
        for bi in range(nb):
            if (bi, hh) not in state:
                state[bi, hh] = s_ref[bi, hh]
            for ci in range(segs_per_batch):
                s = bi * segs_per_batch + ci
                rows = slice(s * chunk, (s + 1) * chunk)
                o_inter.append(_dot(qd_h[rows, :], state[bi, hh].astype(BF16)))
                e_col = jnp.sum(eye * e_last[s][:, ks], axis=1, keepdims=True)
                state[bi, hh] = e_col * state[bi, hh] + kv_all[s * GLA_DK:(s + 1) * GLA_DK, :]
        o = o_intra + jnp.concatenate(o_inter, axis=0)
        o = o * lax.rsqrt(jnp.mean(o * o, axis=-1, keepdims=True) + EPS) * gnorm
        o_s[:, r0:r0 + sub, vs] = (o * out_gate[:, vs]).astype(BF16).reshape(nb, sub, GLA_DV)
        yield 700.0


def _gla_residual(h_ref, o_s, wo_ref, *, nb, r0, sub):
    ms = nb * sub
    h = h_ref[:, r0:r0 + sub, :].reshape(ms, D_MODEL)
    return h + _dot(o_s[:, r0:r0 + sub, :].reshape(ms, GLA_VAL_DIM), wo_ref[...])


def _gla_layer_kernel(h_ref, p_ref, s0_ref, vec_ref, win_ref, wgu_ref, wo_ref,
                      wg_ref, wu_ref, wd_ref, wpg_ref, wpp_ref,
                      y_out_ref, s_ref, o_s, *, nb, tm, sub, chunk):
    t = pl.program_id(1)

    @pl.when(t == 0)
    def _():
        s_ref[...] = s0_ref[...]

    state = {}
    starts = list(range(0, tm, sub))
    fronts = [_gla_front(h_ref, vec_ref, win_ref, wgu_ref, s_ref, o_s, state,
                         nb=nb, r0=r0, sub=sub, chunk=chunk) for r0 in starts]
    backs = [_ffn_back(functools.partial(_gla_residual, h_ref, o_s, wo_ref, nb=nb, r0=r0, sub=sub),
                       p_ref, vec_ref, wg_ref, wu_ref, wd_ref, wpg_ref, wpp_ref, y_out_ref,
                       nb=nb, r0=r0, sub=sub, final_norm=True) for r0 in starts]
    _pipeline(fronts, backs)
    for (bi, hh), s in state.items():
        s_ref[bi, hh] = s


def _gla_layer(h, p, layer, s0, vecs, win, wgu, wo, wg, wu, wd, wpg, wpp, *, nb, tm, sub):
    B, T, _ = h.shape
    chunk = min(CHUNK, T)
    grid = (B // nb, T // tm)
    kern = functools.partial(_gla_layer_kernel, nb=nb, tm=tm, sub=sub, chunk=chunk)
    weights = (vecs, win, wgu, wo, wg, wu, wd, wpg, wpp)
    wlayer = (0, 0, 0, 0, layer, layer, layer, layer, layer)
    state_spec = pl.BlockSpec((nb, GLA_HEADS, GLA_DK, GLA_DV), lambda b, t: (b, 0, 0, 0))
    return pl.pallas_call(
        kern,
        grid=grid,
        in_specs=[
            pl.BlockSpec((nb, tm, D_MODEL), lambda b, t: (b, t, 0)),
            pl.BlockSpec((None, nb, tm, PLE_DIM), lambda b, t: (layer, b, t, 0)),
            state_spec,
        ] + [_layer_spec(w.shape, l) for w, l in zip(weights, wlayer)],
        out_specs=[
            pl.BlockSpec((nb, tm, D_MODEL), lambda b, t: (b, t, 0)),
            state_spec,
        ],
        out_shape=[
            jax.ShapeDtypeStruct((B, T, D_MODEL), F32),
            jax.ShapeDtypeStruct((B, GLA_HEADS, GLA_DK, GLA_DV), F32),
        ],
        scratch_shapes=[
            pltpu.VMEM((nb, tm, GLA_VAL_DIM), BF16),
        ],
        compiler_params=pltpu.CompilerParams(
            dimension_semantics=("arbitrary", "arbitrary"),
            vmem_limit_bytes=VMEM_LIMIT_BYTES),
        name="gla_layer",
    )(h, p, s0, *weights)


def _pack_rows(rows):
    out = [jnp.pad(r.astype(F32), (0, D_MODEL - r.shape[0])) for r in rows]
    out += [jnp.zeros((D_MODEL,), F32)] * (VEC_ROWS - len(out))
    return jnp.stack(out)


def kernel(x_prompt, x_sample, state_pool, state_gla, p_prompt, p_sample, norm_mix, norm_ffn,
           norm_ple, norm_final, w_pool, b_pool, pool_scale, w_gla_in, w_gla_gate_up, b_gla_gate,
           gla_norm, w_gla_out, w_ffn_gate, w_ffn_up, w_ffn_down, w_ple_proj, w_ple_gate):
    vec0 = _pack_rows([norm_mix[0], norm_ffn[0], norm_ple[0], b_pool[0], pool_scale[0]])[None]
    vec1 = _pack_rows([norm_mix[1], norm_ffn[1], norm_ple[1], b_gla_gate[0], gla_norm[0],
                       norm_final])[None]
    wpool = w_pool.astype(BF16)
    ffn = (w_ffn_gate.astype(BF16), w_ffn_up.astype(BF16), w_ffn_down.astype(BF16),
           w_ple_gate.astype(BF16), w_ple_proj.astype(BF16))
    rank_pad = GATE_RANK_PAD - GLA_GATE_RANK
    win = jnp.pad(w_gla_in.astype(BF16), ((0, 0), (0, 0), (0, rank_pad)))
    wgu = jnp.pad(w_gla_gate_up.astype(BF16), ((0, 0), (0, rank_pad), (0, 0)))
    wo = w_gla_out.astype(BF16)

    bp = x_prompt.shape[0]
    hist0 = jnp.zeros((bp, POOL_HIST, D_MODEL), F32)
    s00 = jnp.zeros((bp, GLA_HEADS, GLA_DK, GLA_DV), F32)

    def trunk(x, p, hist, s0, start, tile0, tile1):
        h1, pool_state = _pool_layer(x, p, 0, hist, vec0, wpool, *ffn, start=start, **tile0)
        y, gla_state = _gla_layer(h1, p, 1, s0, vec1, win, wgu, wo, *ffn, **tile1)
        return y, pool_state[None], gla_state[None]

    prompt_tile = dict(nb=1, tm=PROMPT_TILE, sub=PROMPT_SUB)
    y_p, ps_p, gs_p = trunk(x_prompt, p_prompt, hist0, s00, 0, prompt_tile, prompt_tile)
    bs, ts = x_sample.shape[:2]
    y_s, ps_s, gs_s = trunk(x_sample, p_sample, state_pool[0], state_gla[0], PAST_LEN,
                            dict(nb=bs, tm=ts, sub=ts), dict(nb=bs // 2, tm=ts, sub=ts))
    return (y_p, y_s, ps_p, ps_s, gs_p, gs_s)
```

```python
import functools

import jax
import jax.numpy as jnp
from jax import lax
from jax.experimental import pallas as pl
from jax.experimental.pallas import tpu as pltpu

D_MODEL = 1024
PAST_LEN = 2048
CHUNK = 64
POOL_WINDOWS = (2, 4, 8, 16)
POOL_GROUP = D_MODEL // len(POOL_WINDOWS)
POOL_HIST = max(POOL_WINDOWS) - 1
HIST_PAD = POOL_HIST + 1
GLA_HEADS = 4
GLA_KEY_DIM = D_MODEL // 2
GLA_VAL_DIM = D_MODEL
GLA_DK = GLA_KEY_DIM // GLA_HEADS
GLA_DV = GLA_VAL_DIM // GLA_HEADS
GLA_GATE_RANK = 16
GLA_GATE_NORMALIZER = 16.0
GATE_RANK_PAD = 128
IN_Q, IN_K, IN_V = 0, GLA_KEY_DIM, 2 * GLA_KEY_DIM
IN_G = IN_V + GLA_VAL_DIM
IN_R = IN_G + GLA_VAL_DIM
IN_END = IN_R + GATE_RANK_PAD
D_FF = 2816
PLE_DIM = 256
EPS = 1e-6

VMEM_LIMIT_BYTES = 58 * 1024 * 1024
FFN_CHUNK = 256
PROMPT_SUB = 256
PROMPT_TILE = 4 * PROMPT_SUB

BF16 = jnp.bfloat16
F32 = jnp.float32

V_NORM_MIX, V_NORM_FFN, V_NORM_PLE, V_A, V_B, V_C = 0, 1, 2, 3, 4, 5
VEC_ROWS = 8


def _dot(a, b):
    return jnp.dot(a, b, preferred_element_type=F32)


def _rms(x, g):
    ms = jnp.mean(x * x, axis=-1, keepdims=True)
    return x * lax.rsqrt(ms + EPS) * g


def _run(gen):
    for _ in gen:
        pass


def _interleave(a, b):
    total_a, total_b = next(a), next(b)
    done_a = done_b = 0.0
    live_a = live_b = True
    while live_a or live_b:
        pick_a = live_a and (not live_b or done_a / total_a <= done_b / total_b)
        try:
            if pick_a:
                done_a += next(a)
            else:
                done_b += next(b)
        except StopIteration:
            if pick_a:
                live_a = False
            else:
                live_b = False


def _pipeline(fronts, backs):
    _run(fronts[0])
    for i in range(1, len(fronts)):
        _interleave(backs[i - 1], fronts[i])
    _run(backs[-1])


class _Stream:
    def __init__(self, gen, start, span):
        self.gen = gen
        self.total = next(gen)
        self.start, self.span = start, span
        self.done = 0.0
        self.live = True

    def clock(self):
        return self.start + self.span * self.done / self.total

    def step(self):
        try:
            self.done += next(self.gen)
        except StopIteration:
            self.live = False


def _pipeline_staggered(fronts, backs):
    first = _Stream(fronts[0], 0.0, 0.0)
    first.span = first.total
    streams = [first]
    unit = None
    for i, back in enumerate(backs):
        sb = _Stream(back, 0.0, 0.0)
        unit = sb.total if unit is None else unit
        sb.start, sb.span = first.total + i * unit, 2.0 * unit
        streams.append(sb)
        if i + 1 < len(fronts):
            streams.append(_Stream(fronts[i + 1], first.total + i * unit, unit))
    while True:
        live = [s for s in streams if s.live]
        if not live:
            break
        min(live, key=_Stream.clock).step()


def _ffn_back(residual, p_ref, vec_ref, wg_ref, wu_ref, wd_ref, wpg_ref, wpp_ref, out_ref,
              *, nb, r0, sub, final_norm):
    ms = nb * sub
    yield 11950.0
    h = residual()
    yield 1024.0
    xn = _rms(h, vec_ref[V_NORM_FFN:V_NORM_FFN + 1, :]).astype(BF16)
    yield 400.0
    acc = None
    cols = list(range(0, D_FF, FFN_CHUNK))
    nxt = (_dot(xn, wg_ref[:, 0:FFN_CHUNK]), _dot(xn, wu_ref[:, 0:FFN_CHUNK]))
    for i, c0 in enumerate(cols):
        gate, up = nxt
        if i + 1 < len(cols):
            c1 = cols[i + 1]
            nxt = (_dot(xn, wg_ref[:, c1:c1 + FFN_CHUNK]), _dot(xn, wu_ref[:, c1:c1 + FFN_CHUNK]))
        act = (gate * jax.nn.sigmoid(gate) * up).astype(BF16)
        part = _dot(act, wd_ref[c0:c0 + FFN_CHUNK, :])
        acc = part if acc is None else acc + part
        yield 768.0
    h = h + acc
    xn = _rms(h, vec_ref[V_NORM_PLE:V_NORM_PLE + 1, :]).astype(BF16)
    yield 400.0
    ple_gate = jax.nn.sigmoid(_dot(xn, wpg_ref[...]))
    yield 1024.0
    p_bf = p_ref[:, r0:r0 + sub, :].reshape(ms, PLE_DIM).astype(BF16)
    h = h + ple_gate * _dot(p_bf, wpp_ref[...])
    yield 256.0
    if final_norm:
        h = _rms(h, vec_ref[V_C:V_C + 1, :])
    out_ref[:, r0:r0 + sub, :] = h.reshape(nb, sub, D_MODEL)
    yield 400.0


def _pool_front(x_ref, vec_ref, wpool_ref, buf_ref, d_ref, carry, *, nb, r0, sub, pos0):
    ms = nb * sub
    yield 2400.0
    x = x_ref[:, r0:r0 + sub, :].reshape(ms, D_MODEL)
    xn = _rms(x, vec_ref[V_NORM_MIX:V_NORM_MIX + 1, :])
    buf_ref[:, HIST_PAD + r0:HIST_PAD + r0 + sub, :] = xn.reshape(nb, sub, D_MODEL)
    yield 600.0
    pos = pos0 + r0 + lax.broadcasted_iota(jnp.int32, (sub, 1), 0)
    d0 = r0 * nb
    ys = []
    for gi, w in enumerate(POOL_WINDOWS):
        sl = slice(gi * POOL_GROUP, (gi + 1) * POOL_GROUP)
        inv_cnt = 1.0 / jnp.minimum(w, pos + 1).astype(F32)
        for bi in range(nb):
            ext = buf_ref[bi, r0:r0 + HIST_PAD + sub, sl]
            acc = ext
            span = 1
            while span < w:
                acc = acc + pltpu.roll(acc, span, 0)
                span *= 2
            cur = ext[HIST_PAD:, :]
            d = acc[HIST_PAD:, :] * inv_cnt - cur
            d_ref[d0 + bi * sub:d0 + (bi + 1) * sub, sl] = d.astype(BF16)
        ys.append(_dot(d_ref[d0:d0 + ms, sl], wpool_ref[gi]))
        yield 100.0 * (gi + 2)
    y = jnp.concatenate(ys, axis=1)
    y = (y + vec_ref[V_A:V_A + 1, :]) * vec_ref[V_B:V_B + 1, :]
    carry[r0] = x + y
    yield 400.0


def _pool_layer_kernel(x_ref, p_ref, hist_ref, vec_ref, wpool_ref, wg_ref, wu_ref, wd_ref,
                       wpg_ref, wpp_ref, h_out_ref, state_out_ref, buf_ref, d_ref,
                       *, nb, tm, sub, start):
    t = pl.program_id(1)

    @pl.when(t == 0)
    def _():
        buf_ref[:, 0:1, :] = jnp.zeros((nb, 1, D_MODEL), F32)
        buf_ref[:, 1:HIST_PAD, :] = hist_ref[...]

    carry = {}
    starts = list(range(0, tm, sub))
    fronts = [_pool_front(x_ref, vec_ref, wpool_ref, buf_ref, d_ref, carry,
                          nb=nb, r0=r0, sub=sub, pos0=start + t * tm) for r0 in starts]
    backs = [_ffn_back(functools.partial(carry.get, r0), p_ref, vec_ref, wg_ref, wu_ref, wd_ref,
                       wpg_ref, wpp_ref, h_out_ref, nb=nb, r0=r0, sub=sub, final_norm=False)
             for r0 in starts]
    _pipeline_staggered(fronts, backs)

    @pl.when(t == pl.num_programs(1) - 1)
    def _():
        state_out_ref[...] = buf_ref[:, tm + 1:tm + HIST_PAD, :]

    buf_ref[:, 0:HIST_PAD, :] = buf_ref[:, tm:tm + HIST_PAD, :]


def _layer_spec(shape, layer):
    nd = len(shape) - 1
    return pl.BlockSpec((None,) + tuple(shape[1:]), lambda b, t: (layer,) + (0,) * nd,
                        pipeline_mode=pl.Buffered(1))


def _pool_layer(x, p, layer, hist, vecs, wpool, wg, wu, wd, wpg, wpp, *, nb, tm, sub, start):
    weights = (vecs, wpool, wg, wu, wd, wpg, wpp)
    wlayer = (0, 0, layer, layer, layer, layer, layer)
    B, T, _ = x.shape
    grid = (B // nb, T // tm)
    kern = functools.partial(_pool_layer_kernel, nb=nb, tm=tm, sub=sub, start=start)
    return pl.pallas_call(
        kern,
        grid=grid,
        in_specs=[
            pl.BlockSpec((nb, tm, D_MODEL), lambda b, t: (b, t, 0)),
            pl.BlockSpec((None, nb, tm, PLE_DIM), lambda b, t: (layer, b, t, 0)),
            pl.BlockSpec((nb, POOL_HIST, D_MODEL), lambda b, t: (b, 0, 0)),
        ] + [_layer_spec(w.shape, l) for w, l in zip(weights, wlayer)],
        out_specs=[
            pl.BlockSpec((nb, tm, D_MODEL), lambda b, t: (b, t, 0)),
            pl.BlockSpec((nb, POOL_HIST, D_MODEL), lambda b, t: (b, 0, 0)),
        ],
        out_shape=[
            jax.ShapeDtypeStruct((B, T, D_MODEL), F32),
            jax.ShapeDtypeStruct((B, POOL_HIST, D_MODEL), F32),
        ],
        scratch_shapes=[
            pltpu.VMEM((nb, HIST_PAD + tm, D_MODEL), F32),
            pltpu.VMEM((nb * tm, D_MODEL), BF16),
        ],
        compiler_params=pltpu.CompilerParams(
            dimension_semantics=("arbitrary", "arbitrary"),
            vmem_limit_bytes=VMEM_LIMIT_BYTES),
        name="pool_layer",
    )(x, p, hist, *weights)


def _gla_front(h_ref, vec_ref, win_ref, wgu_ref, s_ref, o_s, state, *, nb, r0, sub, chunk):
    ms = nb * sub
    segs_per_batch = sub // chunk
    nseg = nb * segs_per_batch
    half = GLA_VAL_DIM // 2
    yield 9700.0
    h = h_ref[:, r0:r0 + sub, :].reshape(ms, D_MODEL)
    xn = _rms(h, vec_ref[V_NORM_MIX:V_NORM_MIX + 1, :]).astype(BF16)
    yield 400.0
    gr = _dot(xn, win_ref[:, IN_R:IN_END]).astype(BF16)
    q = _dot(xn, win_ref[:, IN_Q:IN_K]) * (GLA_DK ** -0.5)
    yield 512.0
    z = _dot(gr, wgu_ref[...]) + vec_ref[V_A:V_A + 1, 0:GLA_KEY_DIM]
    log_sig = jnp.minimum(z, 0.0) - jnp.log1p(jnp.exp(-jnp.abs(z)))
    la = log_sig * (1.0 / GLA_GATE_NORMALIZER)
    la_hi = la.astype(BF16)
    la_lo = (la - la_hi.astype(F32)).astype(BF16)
    k = _dot(xn, win_ref[:, IN_K:IN_V])
    yield 512.0
    row = lax.broadcasted_iota(jnp.int32, (ms, ms), 0)
    col = lax.broadcasted_iota(jnp.int32, (ms, ms), 1)
    causal = ((row & (-chunk)) == (col & (-chunk))) & (col <= row)
    tri = causal.astype(BF16)
    b = _dot(tri, la_hi) + _dot(tri, la_lo)
    v_parts = []
    for c0 in (IN_V, IN_V + half):
        v_parts.append(_dot(xn, win_ref[:, c0:c0 + half]).astype(BF16))
        yield 512.0
    v = jnp.concatenate(v_parts, axis=1)
    b_last = [b[(s + 1) * chunk - 1:(s + 1) * chunk, :] for s in range(nseg)]
    b_last_rows = jnp.concatenate(
        [jnp.broadcast_to(bl, (chunk, GLA_KEY_DIM)) for bl in b_last], axis=0)
    q_dec = (q * jnp.exp(b)).astype(BF16)
    k_inv = (k * jnp.exp(-b)).astype(BF16)
    k_end = (k * jnp.exp(b_last_rows - b)).astype(BF16)
    e_last = [jnp.exp(bl) for bl in b_last]
    yield 300.0
    g_parts = []
    for c0 in (IN_G, IN_G + half):
        g = _dot(xn, win_ref[:, c0:c0 + half])
        g_parts.append(g * jax.nn.sigmoid(g))
        yield 512.0
    out_gate = jnp.concatenate(g_parts, axis=1)

    eye = (lax.broadcasted_iota(jnp.int32, (GLA_DK, GLA_DK), 0)
           == lax.broadcasted_iota(jnp.int32, (GLA_DK, GLA_DK), 1)).astype(F32)
    gnorm = vec_ref[V_B:V_B + 1, 0:GLA_DV]
    row_seg = lax.broadcasted_iota(jnp.int32, (ms, GLA_DK), 0) & (-chunk)

    for hh in range(GLA_HEADS):
        ks = slice(hh * GLA_DK, (hh + 1) * GLA_DK)
        vs = slice(hh * GLA_DV, (hh + 1) * GLA_DV)
        qd_h = q_dec[:, ks]
        v_h = v[:, vs]
        scores = lax.dot_general(qd_h, k_inv[:, ks], (((1,), (1,)), ((), ())),
                                 preferred_element_type=F32)
        scores = jnp.where(causal, scores, 0.0).astype(BF16)
        o_intra = _dot(scores, v_h)
        ke_h = k_end[:, ks]
        ke_blk = jnp.concatenate(
            [jnp.where(row_seg == s * chunk, ke_h, jnp.zeros_like(ke_h)) for s in range(nseg)],
            axis=1)
        kv_all = lax.dot_general(ke_blk, v_h, (((0,), (0,)), ((), ())), preferred_element_type=F32)
        yield 700.0
        o_inter = []
        for bi in range(nb):
            if (bi, hh) not in state:
                state[bi, hh] = s_ref[bi, hh]
            for ci in range(segs_per_batch):
                s = bi * segs_per_batch + ci
                rows = slice(s * chunk, (s + 1) * chunk)
                o_inter.append(_dot(qd_h[rows, :], state[bi, hh].astype(BF16)))
                e_col = jnp.sum(eye * e_last[s][:, ks], axis=1, keepdims=True)
                state[bi, hh] = e_col * state[bi, hh] + kv_all[s * GLA_DK:(s + 1) * GLA_DK, :]
        o = o_intra + jnp.concatenate(o_inter, axis=0)
        o = o * lax.rsqrt(jnp.mean(o * o, axis=-1, keepdims=True) + EPS) * gnorm
        o_s[:, r0:r0 + sub, vs] = (o * out_gate[:, vs]).astype(BF16).reshape(nb, sub, GLA_DV)
        yield 700.0


def _gla_residual(h_ref, o_s, wo_ref, *, nb, r0, sub):
    ms = nb * sub
    h = h_ref[:, r0:r0 + sub, :].reshape(ms, D_MODEL)
    return h + _dot(o_s[:, r0:r0 + sub, :].reshape(ms, GLA_VAL_DIM), wo_ref[...])


def _gla_layer_kernel(h_ref, p_ref, s0_ref, vec_ref, win_ref, wgu_ref, wo_ref,
                      wg_ref, wu_ref, wd_ref, wpg_ref, wpp_ref,
                      y_out_ref, s_ref, o_s, *, nb, tm, sub, chunk):
    t = pl.program_id(1)

    @pl.when(t == 0)
    def _():
        s_ref[...] = s0_ref[...]

    state = {}
    starts = list(range(0, tm, sub))
    fronts = [_gla_front(h_ref, vec_ref, win_ref, wgu_ref, s_ref, o_s, state,
                         nb=nb, r0=r0, sub=sub, chunk=chunk) for r0 in starts]
    backs = [_ffn_back(functools.partial(_gla_residual, h_ref, o_s, wo_ref, nb=nb, r0=r0, sub=sub),
                       p_ref, vec_ref, wg_ref, wu_ref, wd_ref, wpg_ref, wpp_ref, y_out_ref,
                       nb=nb, r0=r0, sub=sub, final_norm=True) for r0 in starts]
    _pipeline(fronts, backs)
    for (bi, hh), s in state.items():
        s_ref[bi, hh] = s


def _gla_layer(h, p, layer, s0, vecs, win, wgu, wo, wg, wu, wd, wpg, wpp, *, nb, tm, sub):
    B, T, _ = h.shape
    chunk = min(CHUNK, T)
    grid = (B // nb, T // tm)
    kern = functools.partial(_gla_layer_kernel, nb=nb, tm=tm, sub=sub, chunk=chunk)
    weights = (vecs, win, wgu, wo, wg, wu, wd, wpg, wpp)
    wlayer = (0, 0, 0, 0, layer, layer, layer, layer, layer)
    state_block = (nb, GLA_HEADS, GLA_DK, GLA_DV)
    state_spec = pl.BlockSpec(state_block, lambda b, t: (b, 0, 0, 0))
    return pl.pallas_call(
        kern,
        grid=grid,
        in_specs=[
            pl.BlockSpec((nb, tm, D_MODEL), lambda b, t: (b, t, 0)),
            pl.BlockSpec((None, nb, tm, PLE_DIM), lambda b, t: (layer, b, t, 0)),
            pl.BlockSpec(state_block, lambda b, t: (b, 0, 0, 0), pipeline_mode=pl.Buffered(1)),
        ] + [_layer_spec(w.shape, l) for w, l in zip(weights, wlayer)],
        out_specs=[
            pl.BlockSpec((nb, tm, D_MODEL), lambda b, t: (b, t, 0)),
            state_spec,
        ],
        out_shape=[
            jax.ShapeDtypeStruct((B, T, D_MODEL), F32),
            jax.ShapeDtypeStruct((B, GLA_HEADS, GLA_DK, GLA_DV), F32),
        ],
        scratch_shapes=[
            pltpu.VMEM((nb, tm, GLA_VAL_DIM), BF16),
        ],
        compiler_params=pltpu.CompilerParams(
            dimension_semantics=("arbitrary", "arbitrary"),
            vmem_limit_bytes=VMEM_LIMIT_BYTES),
        name="gla_layer",
    )(h, p, s0, *weights)


def _pack_rows(rows):
    out = [jnp.pad(r.astype(F32), (0, D_MODEL - r.shape[0])) for r in rows]
    out += [jnp.zeros((D_MODEL,), F32)] * (VEC_ROWS - len(out))
    return jnp.stack(out)


def kernel(x_prompt, x_sample, state_pool, state_gla, p_prompt, p_sample, norm_mix, norm_ffn,
           norm_ple, norm_final, w_pool, b_pool, pool_scale, w_gla_in, w_gla_gate_up, b_gla_gate,
           gla_norm, w_gla_out, w_ffn_gate, w_ffn_up, w_ffn_down, w_ple_proj, w_ple_gate):
    vec0 = _pack_rows([norm_mix[0], norm_ffn[0], norm_ple[0], b_pool[0], pool_scale[0]])[None]
    vec1 = _pack_rows([norm_mix[1], norm_ffn[1], norm_ple[1], b_gla_gate[0], gla_norm[0],
                       norm_final])[None]
    wpool = w_pool.astype(BF16)
    ffn = (w_ffn_gate.astype(BF16), w_ffn_up.astype(BF16), w_ffn_down.astype(BF16),
           w_ple_gate.astype(BF16), w_ple_proj.astype(BF16))
    rank_pad = GATE_RANK_PAD - GLA_GATE_RANK
    win = jnp.pad(w_gla_in.astype(BF16), ((0, 0), (0, 0), (0, rank_pad)))
    wgu = jnp.pad(w_gla_gate_up.astype(BF16), ((0, 0), (0, rank_pad), (0, 0)))
    wo = w_gla_out.astype(BF16)

    bp = x_prompt.shape[0]
    hist0 = jnp.zeros((bp, POOL_HIST, D_MODEL), F32)
    s00 = jnp.zeros((bp, GLA_HEADS, GLA_DK, GLA_DV), F32)

    def trunk(x, p, hist, s0, start, tile0, tile1):
        h1, pool_state = _pool_layer(x, p, 0, hist, vec0, wpool, *ffn, start=start, **tile0)
        y, gla_state = _gla_layer(h1, p, 1, s0, vec1, win, wgu, wo, *ffn, **tile1)
        return y, pool_state[None], gla_state[None]

    prompt_tile = dict(nb=1, tm=PROMPT_TILE, sub=PROMPT_SUB)
    y_p, ps_p, gs_p = trunk(x_prompt, p_prompt, hist0, s00, 0, prompt_tile, prompt_tile)
    bs, ts = x_sample.shape[:2]
    y_s, ps_s, gs_s = trunk(x_sample, p_sample, state_pool[0], state_gla[0], PAST_LEN,
                            dict(nb=bs, tm=ts, sub=ts), dict(nb=bs // 2, tm=ts, sub=ts))
    return (y_p, y_s, ps_p, ps_s, gs_p, gs_s)
```

```python
import functools

import jax
import jax.numpy as jnp
from jax import lax
from jax.experimental import pallas as pl
from jax.experimental.pallas import tpu as pltpu

D_MODEL = 1024
PAST_LEN = 2048
CHUNK = 64
POOL_WINDOWS = (2, 4, 8, 16)
POOL_GROUP = D_MODEL // len(POOL_WINDOWS)
POOL_HIST = max(POOL_WINDOWS) - 1
HIST_PAD = POOL_HIST + 1
GLA_HEADS = 4
GLA_KEY_DIM = D_MODEL // 2
GLA_VAL_DIM = D_MODEL
GLA_DK = GLA_KEY_DIM // GLA_HEADS
GLA_DV = GLA_VAL_DIM // GLA_HEADS
GLA_GATE_RANK = 16
GLA_GATE_NORMALIZER = 16.0
GATE_RANK_PAD = 128
IN_Q, IN_K, IN_V = 0, GLA_KEY_DIM, 2 * GLA_KEY_DIM
IN_G = IN_V + GLA_VAL_DIM
IN_R = IN_G + GLA_VAL_DIM
IN_END = IN_R + GATE_RANK_PAD
D_FF = 2816
PLE_DIM = 256
EPS = 1e-6

VMEM_LIMIT_BYTES = 58 * 1024 * 1024
FFN_CHUNK = 256
PROMPT_SUB = 256
PROMPT_TILE = 4 * PROMPT_SUB

BF16 = jnp.bfloat16
F32 = jnp.float32
BF16_SUBLANES = 16
N_POOL_IN = 10

V_NORM_MIX, V_NORM_FFN, V_NORM_PLE, V_A, V_B, V_C = 0, 1, 2, 3, 4, 5
VEC_ROWS = 8


def _dot(a, b):
    return jnp.dot(a, b, preferred_element_type=F32)


def _rms(x, g):
    ms = jnp.mean(x * x, axis=-1, keepdims=True)
    return x * lax.rsqrt(ms + EPS) * g


def _run(gen):
    for _ in gen:
        pass


def _interleave(a, b):
    total_a, total_b = next(a), next(b)
    done_a = done_b = 0.0
    live_a = live_b = True
    while live_a or live_b:
        pick_a = live_a and (not live_b or done_a / total_a <= done_b / total_b)
        try:
            if pick_a:
                done_a += next(a)
            else:
                done_b += next(b)
        except StopIteration:
            if pick_a:
                live_a = False
            else:
                live_b = False


def _pipeline(fronts, backs):
    _run(fronts[0])
    for i in range(1, len(fronts)):
        _interleave(backs[i - 1], fronts[i])
    _run(backs[-1])


class _Stream:
    def __init__(self, gen, start, span):
        self.gen = gen
        self.total = next(gen)
        self.start, self.span = start, span
        self.done = 0.0
        self.live = True

    def clock(self):
        return self.start + self.span * self.done / self.total

    def step(self):
        try:
            self.done += next(self.gen)
        except StopIteration:
            self.live = False


def _pipeline_staggered(fronts, backs):
    first = _Stream(fronts[0], 0.0, 0.0)
    first.span = first.total
    streams = [first]
    unit = None
    for i, back in enumerate(backs):
        sb = _Stream(back, 0.0, 0.0)
        unit = sb.total if unit is None else unit
        sb.start, sb.span = first.total + i * unit, 2.0 * unit
        streams.append(sb)
        if i + 1 < len(fronts):
            streams.append(_Stream(fronts[i + 1], first.total + i * unit, unit))
    while True:
        live = [s for s in streams if s.live]
        if not live:
            break
        min(live, key=_Stream.clock).step()


def _ffn_back(residual, p_ref, vec_ref, wg_ref, wu_ref, wd_ref, wpg_ref, wpp_ref, out_ref,
              *, nb, r0, sub, final_norm):
    ms = nb * sub
    yield 11950.0
    h = residual()
    yield 1024.0
    xn = _rms(h, vec_ref[V_NORM_FFN:V_NORM_FFN + 1, :]).astype(BF16)
    yield 400.0
    acc = None
    cols = list(range(0, D_FF, FFN_CHUNK))
    nxt = (_dot(xn, wg_ref[:, 0:FFN_CHUNK]), _dot(xn, wu_ref[:, 0:FFN_CHUNK]))
    for i, c0 in enumerate(cols):
        gate, up = nxt
        if i + 1 < len(cols):
            c1 = cols[i + 1]
            nxt = (_dot(xn, wg_ref[:, c1:c1 + FFN_CHUNK]), _dot(xn, wu_ref[:, c1:c1 + FFN_CHUNK]))
        act = (gate * jax.nn.sigmoid(gate) * up).astype(BF16)
        part = _dot(act, wd_ref[c0:c0 + FFN_CHUNK, :])
        acc = part if acc is None else acc + part
        yield 768.0
    h = h + acc
    xn = _rms(h, vec_ref[V_NORM_PLE:V_NORM_PLE + 1, :]).astype(BF16)
    yield 400.0
    ple_gate = jax.nn.sigmoid(_dot(xn, wpg_ref[...]))
    yield 1024.0
    p_bf = p_ref[:, r0:r0 + sub, :].reshape(ms, PLE_DIM).astype(BF16)
    h = h + ple_gate * _dot(p_bf, wpp_ref[...])
    yield 256.0
    if final_norm:
        h = _rms(h, vec_ref[V_C:V_C + 1, :])
    out_ref[:, r0:r0 + sub, :] = h.reshape(nb, sub, D_MODEL)
    yield 400.0


def _pool_front(x_ref, vec_ref, wpool_ref, buf_ref, d_ref, carry, *, nb, r0, sub, pos0):
    ms = nb * sub
    yield 2400.0
    x = x_ref[:, r0:r0 + sub, :].reshape(ms, D_MODEL)
    xn = _rms(x, vec_ref[V_NORM_MIX:V_NORM_MIX + 1, :])
    buf_ref[:, HIST_PAD + r0:HIST_PAD + r0 + sub, :] = xn.reshape(nb, sub, D_MODEL)
    yield 600.0
    pos = pos0 + r0 + lax.broadcasted_iota(jnp.int32, (sub, 1), 0)
    d0 = r0 * nb
    ys = []
    for gi, w in enumerate(POOL_WINDOWS):
        sl = slice(gi * POOL_GROUP, (gi + 1) * POOL_GROUP)
        inv_cnt = 1.0 / jnp.minimum(w, pos + 1).astype(F32)
        for bi in range(nb):
            ext = buf_ref[bi, r0:r0 + HIST_PAD + sub, sl]
            acc = ext
            span = 1
            while span < w:
                acc = acc + pltpu.roll(acc, span, 0)
                span *= 2
            cur = ext[HIST_PAD:, :]
            d = acc[HIST_PAD:, :] * inv_cnt - cur
            d_ref[d0 + bi * sub:d0 + (bi + 1) * sub, sl] = d.astype(BF16)
        ys.append(_dot(d_ref[d0:d0 + ms, sl], wpool_ref[gi]))
        yield 100.0 * (gi + 2)
    y = jnp.concatenate(ys, axis=1)
    y = (y + vec_ref[V_A:V_A + 1, :]) * vec_ref[V_B:V_B + 1, :]
    carry[r0] = x + y
    yield 400.0


def _pool_layer_kernel(*refs, nb, tm, sub, start, n_cast):
    (x_ref, p_ref, hist_ref, vec_ref, wpool_ref, wg_ref, wu_ref, wd_ref, wpg_ref,
     wpp_ref) = refs[:N_POOL_IN]
    cast_in = refs[N_POOL_IN:N_POOL_IN + n_cast]
    h_out_ref, state_out_ref = refs[N_POOL_IN + n_cast:N_POOL_IN + n_cast + 2]
    cast_out = refs[N_POOL_IN + n_cast + 2:N_POOL_IN + 2 * n_cast + 2]
    buf_ref, d_ref = refs[N_POOL_IN + 2 * n_cast + 2:]
    t = pl.program_id(1)

    for src, dst in zip(cast_in, cast_out):
        dst[...] = src[...].astype(BF16)

    @pl.when(t == 0)
    def _():
        buf_ref[:, 0:1, :] = jnp.zeros((nb, 1, D_MODEL), F32)
        buf_ref[:, 1:HIST_PAD, :] = hist_ref[...]

    carry = {}
    starts = list(range(0, tm, sub))
    fronts = [_pool_front(x_ref, vec_ref, wpool_ref, buf_ref, d_ref, carry,
                          nb=nb, r0=r0, sub=sub, pos0=start + t * tm) for r0 in starts]
    backs = [_ffn_back(functools.partial(carry.get, r0), p_ref, vec_ref, wg_ref, wu_ref, wd_ref,
                       wpg_ref, wpp_ref, h_out_ref, nb=nb, r0=r0, sub=sub, final_norm=False)
             for r0 in starts]
    _pipeline_staggered(fronts, backs)

    @pl.when(t == pl.num_programs(1) - 1)
    def _():
        state_out_ref[...] = buf_ref[:, tm + 1:tm + HIST_PAD, :]

    buf_ref[:, 0:HIST_PAD, :] = buf_ref[:, tm:tm + HIST_PAD, :]


def _layer_spec(shape, layer):
    assert layer < shape[0]
    nd = len(shape) - 1
    return pl.BlockSpec((None,) + tuple(shape[1:]), lambda b, t: (layer,) + (0,) * nd,
                        pipeline_mode=pl.Buffered(1))


def _pool_layer(x, p, p_layer, hist, vecs, wpool, wg, wu, wd, wpg, wpp, *, nb, tm, sub, start,
                cast=()):
    weights = (vecs, wpool, wg, wu, wd, wpg, wpp)
    B, T, _ = x.shape
    grid = (B // nb, T // tm)
    steps = grid[0] * grid[1]
    cast_in_specs, cast_out_specs, cast_out_shapes = [], [], []
    for w, layer in cast:
        rows, cols = w.shape[1:]
        assert rows % (steps * BF16_SUBLANES) == 0, (w.shape, steps)
        block = (None, rows // steps, cols)
        cast_in_specs.append(
            pl.BlockSpec(block, lambda b, t, layer=layer: (layer, b * grid[1] + t, 0)))
        cast_out_specs.append(pl.BlockSpec(block, lambda b, t: (0, b * grid[1] + t, 0)))
        cast_out_shapes.append(jax.ShapeDtypeStruct((1, rows, cols), BF16))
    kern = functools.partial(_pool_layer_kernel, nb=nb, tm=tm, sub=sub, start=start,
                             n_cast=len(cast))
    return pl.pallas_call(
        kern,
        grid=grid,
        in_specs=[
            pl.BlockSpec((nb, tm, D_MODEL), lambda b, t: (b, t, 0)),
            pl.BlockSpec((None, nb, tm, PLE_DIM), lambda b, t: (p_layer, b, t, 0)),
            pl.BlockSpec((nb, POOL_HIST, D_MODEL), lambda b, t: (b, 0, 0)),
        ] + [_layer_spec(w.shape, 0) for w in weights] + cast_in_specs,
        out_specs=[
            pl.BlockSpec((nb, tm, D_MODEL), lambda b, t: (b, t, 0)),
            pl.BlockSpec((nb, POOL_HIST, D_MODEL), lambda b, t: (b, 0, 0)),
        ] + cast_out_specs,
        out_shape=[
            jax.ShapeDtypeStruct((B, T, D_MODEL), F32),
            jax.ShapeDtypeStruct((B, POOL_HIST, D_MODEL), F32),
        ] + cast_out_shapes,
        scratch_shapes=[
            pltpu.VMEM((nb, HIST_PAD + tm, D_MODEL), F32),
            pltpu.VMEM((nb * tm, D_MODEL), BF16),
        ],
        compiler_params=pltpu.CompilerParams(
            dimension_semantics=("arbitrary", "arbitrary"),
            vmem_limit_bytes=VMEM_LIMIT_BYTES),
        name="pool_layer",
    )(x, p, hist, *weights, *[w for w, _ in cast])


def _gla_front(h_ref, vec_ref, win_ref, wgu_ref, s_ref, o_s, state, *, nb, r0, sub, chunk):
    ms = nb * sub
    segs_per_batch = sub // chunk
    nseg = nb * segs_per_batch
    half = GLA_VAL_DIM // 2
    yield 9700.0
    h = h_ref[:, r0:r0 + sub, :].reshape(ms, D_MODEL)
    xn = _rms(h, vec_ref[V_NORM_MIX:V_NORM_MIX + 1, :]).astype(BF16)
    yield 400.0
    gr = _dot(xn, win_ref[:, IN_R:IN_END]).astype(BF16)
    q = _dot(xn, win_ref[:, IN_Q:IN_K]) * (GLA_DK ** -0.5)
    yield 512.0
    z = _dot(gr, wgu_ref[...]) + vec_ref[V_A:V_A + 1, 0:GLA_KEY_DIM]
    log_sig = jnp.minimum(z, 0.0) - jnp.log1p(jnp.exp(-jnp.abs(z)))
    la = log_sig * (1.0 / GLA_GATE_NORMALIZER)
    la_hi = la.astype(BF16)
    la_lo = (la - la_hi.astype(F32)).astype(BF16)
    k = _dot(xn, win_ref[:, IN_K:IN_V])
    yield 512.0
    row = lax.broadcasted_iota(jnp.int32, (ms, ms), 0)
    col = lax.broadcasted_iota(jnp.int32, (ms, ms), 1)
    causal = ((row & (-chunk)) == (col & (-chunk))) & (col <= row)
    tri = causal.astype(BF16)
    b = _dot(tri, la_hi) + _dot(tri, la_lo)
    v_parts = []
    for c0 in (IN_V, IN_V + half):
        v_parts.append(_dot(xn, win_ref[:, c0:c0 + half]).astype(BF16))
        yield 512.0
    v = jnp.concatenate(v_parts, axis=1)
    b_last = [b[(s + 1) * chunk - 1:(s + 1) * chunk, :] for s in range(nseg)]
    b_last_rows = jnp.concatenate(
        [jnp.broadcast_to(bl, (chunk, GLA_KEY_DIM)) for bl in b_last], axis=0)
    q_dec = (q * jnp.exp(b)).astype(BF16)
    k_inv = (k * jnp.exp(-b)).astype(BF16)
    k_end = (k * jnp.exp(b_last_rows - b)).astype(BF16)
    e_last = [jnp.exp(bl) for bl in b_last]
    yield 300.0
    g_parts = []
    for c0 in (IN_G, IN_G + half):
        g = _dot(xn, win_ref[:, c0:c0 + half])
        g_parts.append(g * jax.nn.sigmoid(g))
        yield 512.0
    out_gate = jnp.concatenate(g_parts, axis=1)

    eye = (lax.broadcasted_iota(jnp.int32, (GLA_DK, GLA_DK), 0)
           == lax.broadcasted_iota(jnp.int32, (GLA_DK, GLA_DK), 1)).astype(F32)
    gnorm = vec_ref[V_B:V_B + 1, 0:GLA_DV]
    row_seg = lax.broadcasted_iota(jnp.int32, (ms, GLA_DK), 0) & (-chunk)

    for hh in range(GLA_HEADS):
        ks = slice(hh * GLA_DK, (hh + 1) * GLA_DK)
        vs = slice(hh * GLA_DV, (hh + 1) * GLA_DV)
        qd_h = q_dec[:, ks]
        v_h = v[:, vs]
        scores = lax.dot_general(qd_h, k_inv[:, ks], (((1,), (1,)), ((), ())),
                                 preferred_element_type=F32)
        scores = jnp.where(causal, scores, 0.0).astype(BF16)
        o_intra = _dot(scores, v_h)
        ke_h = k_end[:, ks]
        ke_blk = jnp.concatenate(
            [jnp.where(row_seg == s * chunk, ke_h, jnp.zeros_like(ke_h)) for s in range(nseg)],
            axis=1)
        kv_all = lax.dot_general(ke_blk, v_h, (((0,), (0,)), ((), ())), preferred_element_type=F32)
        yield 700.0
        o_inter = []
        for bi in range(nb):
            if (bi, hh) not in state:
                state[bi, hh] = s_ref[bi, hh]
            for ci in range(segs_per_batch):
                s = bi * segs_per_batch + ci
                rows = slice(s * chunk, (s + 1) * chunk)
                o_inter.append(_dot(qd_h[rows, :], state[bi, hh].astype(BF16)))
                e_col = jnp.sum(eye * e_last[s][:, ks], axis=1, keepdims=True)
                state[bi, hh] = e_col * state[bi, hh] + kv_all[s * GLA_DK:(s + 1) * GLA_DK, :]
        o = o_intra + jnp.concatenate(o_inter, axis=0)
        o = o * lax.rsqrt(jnp.mean(o * o, axis=-1, keepdims=True) + EPS) * gnorm
        o_s[:, r0:r0 + sub, vs] = (o * out_gate[:, vs]).astype(BF16).reshape(nb, sub, GLA_DV)
        yield 700.0


def _gla_residual(h_ref, o_s, wo_ref, *, nb, r0, sub):
    ms = nb * sub
    h = h_ref[:, r0:r0 + sub, :].reshape(ms, D_MODEL)
    return h + _dot(o_s[:, r0:r0 + sub, :].reshape(ms, GLA_VAL_DIM), wo_ref[...])


def _gla_layer_kernel(h_ref, p_ref, s0_ref, vec_ref, win_ref, wgu_ref, wo_ref,
                      wg_ref, wu_ref, wd_ref, wpg_ref, wpp_ref,
                      y_out_ref, s_ref, o_s, *, nb, tm, sub, chunk):
    t = pl.program_id(1)

    @pl.when(t == 0)
    def _():
        s_ref[...] = s0_ref[...]

    state = {}
    starts = list(range(0, tm, sub))
    fronts = [_gla_front(h_ref, vec_ref, win_ref, wgu_ref, s_ref, o_s, state,
                         nb=nb, r0=r0, sub=sub, chunk=chunk) for r0 in starts]
    backs = [_ffn_back(functools.partial(_gla_residual, h_ref, o_s, wo_ref, nb=nb, r0=r0, sub=sub),
                       p_ref, vec_ref, wg_ref, wu_ref, wd_ref, wpg_ref, wpp_ref, y_out_ref,
                       nb=nb, r0=r0, sub=sub, final_norm=True) for r0 in starts]
    _pipeline(fronts, backs)
    for (bi, hh), s in state.items():
        s_ref[bi, hh] = s


def _gla_layer(h, p, p_layer, s0, vecs, win, wgu, wo, wg, wu, wd, wpg, wpp, *, nb, tm, sub):
    B, T, _ = h.shape
    chunk = min(CHUNK, T)
    grid = (B // nb, T // tm)
    kern = functools.partial(_gla_layer_kernel, nb=nb, tm=tm, sub=sub, chunk=chunk)
    weights = (vecs, win, wgu, wo, wg, wu, wd, wpg, wpp)
    state_block = (nb, GLA_HEADS, GLA_DK, GLA_DV)
    state_spec = pl.BlockSpec(state_block, lambda b, t: (b, 0, 0, 0))
    return pl.pallas_call(
        kern,
        grid=grid,
        in_specs=[
            pl.BlockSpec((nb, tm, D_MODEL), lambda b, t: (b, t, 0)),
            pl.BlockSpec((None, nb, tm, PLE_DIM), lambda b, t: (p_layer, b, t, 0)),
            pl.BlockSpec(state_block, lambda b, t: (b, 0, 0, 0), pipeline_mode=pl.Buffered(1)),
        ] + [_layer_spec(w.shape, 0) for w in weights],
        out_specs=[
            pl.BlockSpec((nb, tm, D_MODEL), lambda b, t: (b, t, 0)),
            state_spec,
        ],
        out_shape=[
            jax.ShapeDtypeStruct((B, T, D_MODEL), F32),
            jax.ShapeDtypeStruct((B, GLA_HEADS, GLA_DK, GLA_DV), F32),
        ],
        scratch_shapes=[
            pltpu.VMEM((nb, tm, GLA_VAL_DIM), BF16),
        ],
        compiler_params=pltpu.CompilerParams(
            dimension_semantics=("arbitrary", "arbitrary"),
            vmem_limit_bytes=VMEM_LIMIT_BYTES),
        name="gla_layer",
    )(h, p, s0, *weights)


def _pack_rows(rows):
    out = [jnp.pad(r.astype(F32), (0, D_MODEL - r.shape[0])) for r in rows]
    out += [jnp.zeros((D_MODEL,), F32)] * (VEC_ROWS - len(out))
    return jnp.stack(out)


def kernel(x_prompt, x_sample, state_pool, state_gla, p_prompt, p_sample, norm_mix, norm_ffn,
           norm_ple, norm_final, w_pool, b_pool, pool_scale, w_gla_in, w_gla_gate_up, b_gla_gate,
           gla_norm, w_gla_out, w_ffn_gate, w_ffn_up, w_ffn_down, w_ple_proj, w_ple_gate):
    vec0 = _pack_rows([norm_mix[0], norm_ffn[0], norm_ple[0], b_pool[0], pool_scale[0]])[None]
    vec1 = _pack_rows([norm_mix[1], norm_ffn[1], norm_ple[1], b_gla_gate[0], gla_norm[0],
                       norm_final])[None]
    wpool = w_pool.astype(BF16)
    per_layer = (w_ffn_gate, w_ffn_up, w_ffn_down, w_ple_gate, w_ple_proj)
    ffn0 = tuple(w[0:1].astype(BF16) for w in per_layer)
    rank_pad = GATE_RANK_PAD - GLA_GATE_RANK
    win = jnp.pad(w_gla_in.astype(BF16), ((0, 0), (0, 0), (0, rank_pad)))
    wgu = jnp.pad(w_gla_gate_up.astype(BF16), ((0, 0), (0, rank_pad), (0, 0)))

    bp = x_prompt.shape[0]
    hist0 = jnp.zeros((bp, POOL_HIST, D_MODEL), F32)
    s00 = jnp.zeros((bp, GLA_HEADS, GLA_DK, GLA_DV), F32)

    prompt_tile = dict(nb=1, tm=PROMPT_TILE, sub=PROMPT_SUB)
    h1_p, ps_p, *cast = _pool_layer(x_prompt, p_prompt, 0, hist0, vec0, wpool, *ffn0, start=0,
                                    cast=[(w, 1) for w in per_layer] + [(w_gla_out, 0)],
                                    **prompt_tile)
    ffn1, wo = cast[:len(per_layer)], cast[len(per_layer)]
    y_p, gs_p = _gla_layer(h1_p, p_prompt, 1, s00, vec1, win, wgu, wo, *ffn1, **prompt_tile)

    bs, ts = x_sample.shape[:2]
    h1_s, ps_s = _pool_layer(x_sample, p_sample, 0, state_pool[0], vec0, wpool, *ffn0,
                             start=PAST_LEN, nb=bs, tm=ts, sub=ts)
    y_s, gs_s = _gla_layer(h1_s, p_sample, 1, state_gla[0], vec1, win, wgu, wo, *ffn1,
                           nb=bs // 2, tm=ts, sub=ts)
    return (y_p, y_s, ps_p[None], ps_s[None], gs_p[None], gs_s[None])
```

```python
import functools

import jax
import jax.numpy as jnp
from jax import lax
from jax.experimental import pallas as pl
from jax.experimental.pallas import tpu as pltpu

D_MODEL = 1024
PAST_LEN = 2048
CHUNK = 64
POOL_WINDOWS = (2, 4, 8, 16)
POOL_GROUP = D_MODEL // len(POOL_WINDOWS)
POOL_HIST = max(POOL_WINDOWS) - 1
HIST_PAD = POOL_HIST + 1
GLA_HEADS = 4
GLA_KEY_DIM = D_MODEL // 2
GLA_VAL_DIM = D_MODEL
GLA_DK = GLA_KEY_DIM // GLA_HEADS
GLA_DV = GLA_VAL_DIM // GLA_HEADS
GLA_GATE_RANK = 16
GLA_GATE_NORMALIZER = 16.0
GATE_RANK_PAD = 128
IN_Q, IN_K, IN_V = 0, GLA_KEY_DIM, 2 * GLA_KEY_DIM
IN_G = IN_V + GLA_VAL_DIM
IN_R = IN_G + GLA_VAL_DIM
D_FF = 2816
PLE_DIM = 256
EPS = 1e-6

VMEM_LIMIT_BYTES = 58 * 1024 * 1024
FFN_CHUNK = 256
PROMPT_SUB = 256
PROMPT_TILE = 4 * PROMPT_SUB

BF16 = jnp.bfloat16
F32 = jnp.float32

V_NORM_MIX, V_NORM_FFN, V_NORM_PLE, V_A, V_B, V_C = 0, 1, 2, 3, 4, 5
VEC_ROWS = 8


def _dot(a, b):
    return jnp.dot(a, b, preferred_element_type=F32)


def _rms(x, g):
    ms = jnp.mean(x * x, axis=-1, keepdims=True)
    return x * lax.rsqrt(ms + EPS) * g


def _run(gen):
    for _ in gen:
        pass


def _interleave(a, b):
    total_a, total_b = next(a), next(b)
    done_a = done_b = 0.0
    live_a = live_b = True
    while live_a or live_b:
        pick_a = live_a and (not live_b or done_a / total_a <= done_b / total_b)
        try:
            if pick_a:
                done_a += next(a)
            else:
                done_b += next(b)
        except StopIteration:
            if pick_a:
                live_a = False
            else:
                live_b = False


def _pipeline(fronts, backs):
    _run(fronts[0])
    for i in range(1, len(fronts)):
        _interleave(backs[i - 1], fronts[i])
    _run(backs[-1])


class _Stream:
    def __init__(self, gen, start, span):
        self.gen = gen
        self.total = next(gen)
        self.start, self.span = start, span
        self.done = 0.0
        self.live = True

    def clock(self):
        return self.start + self.span * self.done / self.total

    def step(self):
        try:
            self.done += next(self.gen)
        except StopIteration:
            self.live = False


def _pipeline_staggered(fronts, backs):
    first = _Stream(fronts[0], 0.0, 0.0)
    first.span = first.total
    streams = [first]
    unit = None
    for i, back in enumerate(backs):
        sb = _Stream(back, 0.0, 0.0)
        unit = sb.total if unit is None else unit
        sb.start, sb.span = first.total + i * unit, 2.0 * unit
        streams.append(sb)
        if i + 1 < len(fronts):
            streams.append(_Stream(fronts[i + 1], first.total + i * unit, unit))
    while True:
        live = [s for s in streams if s.live]
        if not live:
            break
        min(live, key=_Stream.clock).step()


def _ffn_back(residual, p_ref, vec_ref, wg_ref, wu_ref, wd_ref, wpg_ref, wpp_ref, out_ref,
              *, nb, r0, sub, final_norm):
    ms = nb * sub
    yield 11950.0
    h = residual()
    yield 1024.0
    xn = _rms(h, vec_ref[V_NORM_FFN:V_NORM_FFN + 1, :]).astype(BF16)
    yield 400.0
    acc = None
    cols = list(range(0, D_FF, FFN_CHUNK))
    nxt = (_dot(xn, wg_ref[:, 0:FFN_CHUNK]), _dot(xn, wu_ref[:, 0:FFN_CHUNK]))
    for i, c0 in enumerate(cols):
        gate, up = nxt
        if i + 1 < len(cols):
            c1 = cols[i + 1]
            nxt = (_dot(xn, wg_ref[:, c1:c1 + FFN_CHUNK]), _dot(xn, wu_ref[:, c1:c1 + FFN_CHUNK]))
        act = (gate * jax.nn.sigmoid(gate) * up).astype(BF16)
        part = _dot(act, wd_ref[c0:c0 + FFN_CHUNK, :])
        acc = part if acc is None else acc + part
        yield 768.0
    h = h + acc
    xn = _rms(h, vec_ref[V_NORM_PLE:V_NORM_PLE + 1, :]).astype(BF16)
    yield 400.0
    ple_gate = jax.nn.sigmoid(_dot(xn, wpg_ref[...]))
    yield 1024.0
    p_bf = p_ref[:, r0:r0 + sub, :].reshape(ms, PLE_DIM).astype(BF16)
    h = h + ple_gate * _dot(p_bf, wpp_ref[...])
    yield 256.0
    if final_norm:
        h = _rms(h, vec_ref[V_C:V_C + 1, :])
    out_ref[:, r0:r0 + sub, :] = h.reshape(nb, sub, D_MODEL)
    yield 400.0


def _pool_front(x_ref, vec_ref, wpool_ref, buf_ref, d_ref, carry, *, nb, r0, sub, pos0):
    ms = nb * sub
    yield 2400.0
    x = x_ref[:, r0:r0 + sub, :].reshape(ms, D_MODEL)
    xn = _rms(x, vec_ref[V_NORM_MIX:V_NORM_MIX + 1, :])
    buf_ref[:, HIST_PAD + r0:HIST_PAD + r0 + sub, :] = xn.reshape(nb, sub, D_MODEL)
    yield 600.0
    pos = pos0 + r0 + lax.broadcasted_iota(jnp.int32, (sub, 1), 0)
    d0 = r0 * nb
    ys = []
    for gi, w in enumerate(POOL_WINDOWS):
        sl = slice(gi * POOL_GROUP, (gi + 1) * POOL_GROUP)
        inv_cnt = 1.0 / jnp.minimum(w, pos + 1).astype(F32)
        for bi in range(nb):
            ext = buf_ref[bi, r0:r0 + HIST_PAD + sub, sl]
            acc = ext
            span = 1
            while span < w:
                acc = acc + pltpu.roll(acc, span, 0)
                span *= 2
            cur = ext[HIST_PAD:, :]
            d = acc[HIST_PAD:, :] * inv_cnt - cur
            d_ref[d0 + bi * sub:d0 + (bi + 1) * sub, sl] = d.astype(BF16)
        ys.append(_dot(d_ref[d0:d0 + ms, sl], wpool_ref[gi]))
        yield 100.0 * (gi + 2)
    y = jnp.concatenate(ys, axis=1)
    y = (y + vec_ref[V_A:V_A + 1, :]) * vec_ref[V_B:V_B + 1, :]
    carry[r0] = x + y
    yield 400.0


def _pool_layer_kernel(x_ref, p_ref, hist_ref, vec_ref, wpool_ref, wg_ref, wu_ref, wd_ref,
                       wpg_ref, wpp_ref, h_out_ref, state_out_ref, buf_ref, d_ref,
                       *, nb, tm, sub, start):
    t = pl.program_id(1)

    @pl.when(t == 0)
    def _():
        buf_ref[:, 0:1, :] = jnp.zeros((nb, 1, D_MODEL), F32)
        buf_ref[:, 1:HIST_PAD, :] = hist_ref[...]

    carry = {}
    starts = list(range(0, tm, sub))
    fronts = [_pool_front(x_ref, vec_ref, wpool_ref, buf_ref, d_ref, carry,
                          nb=nb, r0=r0, sub=sub, pos0=start + t * tm) for r0 in starts]
    backs = [_ffn_back(functools.partial(carry.get, r0), p_ref, vec_ref, wg_ref, wu_ref, wd_ref,
                       wpg_ref, wpp_ref, h_out_ref, nb=nb, r0=r0, sub=sub, final_norm=False)
             for r0 in starts]
    _pipeline_staggered(fronts, backs)

    @pl.when(t == pl.num_programs(1) - 1)
    def _():
        state_out_ref[...] = buf_ref[:, tm + 1:tm + HIST_PAD, :]

    buf_ref[:, 0:HIST_PAD, :] = buf_ref[:, tm:tm + HIST_PAD, :]


def _layer_spec(shape, layer):
    nd = len(shape) - 1
    return pl.BlockSpec((None,) + tuple(shape[1:]), lambda b, t: (layer,) + (0,) * nd,
                        pipeline_mode=pl.Buffered(1))


def _pool_layer(x, p, layer, hist, vecs, wpool, wg, wu, wd, wpg, wpp, *, nb, tm, sub, start):
    weights = (vecs, wpool, wg, wu, wd, wpg, wpp)
    wlayer = (0, 0, layer, layer, layer, layer, layer)
    B, T, _ = x.shape
    grid = (B // nb, T // tm)
    kern = functools.partial(_pool_layer_kernel, nb=nb, tm=tm, sub=sub, start=start)
    state_spec = pl.BlockSpec((None, nb, POOL_HIST, D_MODEL), lambda b, t: (0, b, 0, 0))
    return pl.pallas_call(
        kern,
        grid=grid,
        in_specs=[
            pl.BlockSpec((nb, tm, D_MODEL), lambda b, t: (b, t, 0)),
            pl.BlockSpec((None, nb, tm, PLE_DIM), lambda b, t: (layer, b, t, 0)),
            state_spec,
        ] + [_layer_spec(w.shape, l) for w, l in zip(weights, wlayer)],
        out_specs=[
            pl.BlockSpec((nb, tm, D_MODEL), lambda b, t: (b, t, 0)),
            state_spec,
        ],
        out_shape=[
            jax.ShapeDtypeStruct((B, T, D_MODEL), F32),
            jax.ShapeDtypeStruct((1, B, POOL_HIST, D_MODEL), F32),
        ],
        scratch_shapes=[
            pltpu.VMEM((nb, HIST_PAD + tm, D_MODEL), F32),
            pltpu.VMEM((nb * tm, D_MODEL), BF16),
        ],
        compiler_params=pltpu.CompilerParams(
            dimension_semantics=("arbitrary", "arbitrary"),
            vmem_limit_bytes=VMEM_LIMIT_BYTES),
        name="pool_layer",
    )(x, p, hist, *weights)


def _gla_front(h_ref, vec_ref, win_ref, wgr_ref, wgu_ref, s_ref, o_s, state,
               *, nb, r0, sub, chunk):
    ms = nb * sub
    segs_per_batch = sub // chunk
    nseg = nb * segs_per_batch
    half = GLA_VAL_DIM // 2
    yield 9700.0
    h = h_ref[:, r0:r0 + sub, :].reshape(ms, D_MODEL)
    xn = _rms(h, vec_ref[V_NORM_MIX:V_NORM_MIX + 1, :]).astype(BF16)
    yield 400.0
    gr = _dot(xn, wgr_ref[...]).astype(BF16)
    q = _dot(xn, win_ref[:, IN_Q:IN_K]) * (GLA_DK ** -0.5)
    yield 512.0
    z = _dot(gr, wgu_ref[...]) + vec_ref[V_A:V_A + 1, 0:GLA_KEY_DIM]
    log_sig = jnp.minimum(z, 0.0) - jnp.log1p(jnp.exp(-jnp.abs(z)))
    la = log_sig * (1.0 / GLA_GATE_NORMALIZER)
    la_hi = la.astype(BF16)
    la_lo = (la - la_hi.astype(F32)).astype(BF16)
    k = _dot(xn, win_ref[:, IN_K:IN_V])
    yield 512.0
    row = lax.broadcasted_iota(jnp.int32, (ms, ms), 0)
    col = lax.broadcasted_iota(jnp.int32, (ms, ms), 1)
    causal = ((row & (-chunk)) == (col & (-chunk))) & (col <= row)
    tri = causal.astype(BF16)
    b = _dot(tri, la_hi) + _dot(tri, la_lo)
    v_parts = []
    for c0 in (IN_V, IN_V + half):
        v_parts.append(_dot(xn, win_ref[:, c0:c0 + half]).astype(BF16))
        yield 512.0
    v = jnp.concatenate(v_parts, axis=1)
    b_last = [b[(s + 1) * chunk - 1:(s + 1) * chunk, :] for s in range(nseg)]
    b_last_rows = jnp.concatenate(
        [jnp.broadcast_to(bl, (chunk, GLA_KEY_DIM)) for bl in b_last], axis=0)
    q_dec = (q * jnp.exp(b)).astype(BF16)
    k_inv = (k * jnp.exp(-b)).astype(BF16)
    k_end = (k * jnp.exp(b_last_rows - b)).astype(BF16)
    e_last = [jnp.exp(bl) for bl in b_last]
    yield 300.0
    g_parts = []
    for c0 in (IN_G, IN_G + half):
        g = _dot(xn, win_ref[:, c0:c0 + half])
        g_parts.append(g * jax.nn.sigmoid(g))
        yield 512.0
    out_gate = jnp.concatenate(g_parts, axis=1)

    eye = (lax.broadcasted_iota(jnp.int32, (GLA_DK, GLA_DK), 0)
           == lax.broadcasted_iota(jnp.int32, (GLA_DK, GLA_DK), 1)).astype(F32)
    gnorm = vec_ref[V_B:V_B + 1, 0:GLA_DV]
    row_seg = lax.broadcasted_iota(jnp.int32, (ms, GLA_DK), 0) & (-chunk)

    for hh in range(GLA_HEADS):
        ks = slice(hh * GLA_DK, (hh + 1) * GLA_DK)
        vs = slice(hh * GLA_DV, (hh + 1) * GLA_DV)
        qd_h = q_dec[:, ks]
        v_h = v[:, vs]
        scores = lax.dot_general(qd_h, k_inv[:, ks], (((1,), (1,)), ((), ())),
                                 preferred_element_type=F32)
        scores = jnp.where(causal, scores, 0.0).astype(BF16)
        o_intra = _dot(scores, v_h)
        ke_h = k_end[:, ks]
        ke_blk = jnp.concatenate(
            [jnp.where(row_seg == s * chunk, ke_h, jnp.zeros_like(ke_h)) for s in range(nseg)],
            axis=1)
        kv_all = lax.dot_general(ke_blk, v_h, (((0,), (0,)), ((), ())), preferred_element_type=F32)
        yield 700.0
        o_inter = []
        for bi in range(nb):
            if (bi, hh) not in state:
                state[bi, hh] = s_ref[bi, hh]
            for ci in range(segs_per_batch):
                s = bi * segs_per_batch + ci
                rows = slice(s * chunk, (s + 1) * chunk)
                o_inter.append(_dot(qd_h[rows, :], state[bi, hh].astype(BF16)))
                e_col = jnp.sum(eye * e_last[s][:, ks], axis=1, keepdims=True)
                state[bi, hh] = e_col * state[bi, hh] + kv_all[s * GLA_DK:(s + 1) * GLA_DK, :]
        o = o_intra + jnp.concatenate(o_inter, axis=0)
        o = o * lax.rsqrt(jnp.mean(o * o, axis=-1, keepdims=True) + EPS) * gnorm
        o_s[:, r0:r0 + sub, vs] = (o * out_gate[:, vs]).astype(BF16).reshape(nb, sub, GLA_DV)
        yield 700.0


def _gla_residual(h_ref, o_s, wo_ref, *, nb, r0, sub):
    ms = nb * sub
    h = h_ref[:, r0:r0 + sub, :].reshape(ms, D_MODEL)
    return h + _dot(o_s[:, r0:r0 + sub, :].reshape(ms, GLA_VAL_DIM), wo_ref[...])


def _gla_layer_kernel(h_ref, p_ref, s0_ref, vec_ref, win_ref, wgr_ref, wgu_ref, wo_ref,
                      wg_ref, wu_ref, wd_ref, wpg_ref, wpp_ref,
                      y_out_ref, s_ref, o_s, *, nb, tm, sub, chunk):
    t = pl.program_id(1)

    @pl.when(t == 0)
    def _():
        s_ref[...] = s0_ref[...]

    state = {}
    starts = list(range(0, tm, sub))
    fronts = [_gla_front(h_ref, vec_ref, win_ref, wgr_ref, wgu_ref, s_ref, o_s, state,
                         nb=nb, r0=r0, sub=sub, chunk=chunk) for r0 in starts]
    backs = [_ffn_back(functools.partial(_gla_residual, h_ref, o_s, wo_ref, nb=nb, r0=r0, sub=sub),
                       p_ref, vec_ref, wg_ref, wu_ref, wd_ref, wpg_ref, wpp_ref, y_out_ref,
                       nb=nb, r0=r0, sub=sub, final_norm=True) for r0 in starts]
    _pipeline(fronts, backs)
    for (bi, hh), s in state.items():
        s_ref[bi, hh] = s


def _gla_layer(h, p, layer, s0, vecs, win, wgr, wgu, wo, wg, wu, wd, wpg, wpp, *, nb, tm, sub):
    B, T, _ = h.shape
    chunk = min(CHUNK, T)
    grid = (B // nb, T // tm)
    kern = functools.partial(_gla_layer_kernel, nb=nb, tm=tm, sub=sub, chunk=chunk)
    weights = (vecs, win, wgr, wgu, wo, wg, wu, wd, wpg, wpp)
    wlayer = (0, 0, 0, 0, 0, layer, layer, layer, layer, layer)
    state_block = (None, nb, GLA_HEADS, GLA_DK, GLA_DV)

    def state_index(b, t):
        return (0, b, 0, 0, 0)

    return pl.pallas_call(
        kern,
        grid=grid,
        in_specs=[
            pl.BlockSpec((nb, tm, D_MODEL), lambda b, t: (b, t, 0)),
            pl.BlockSpec((None, nb, tm, PLE_DIM), lambda b, t: (layer, b, t, 0)),
            pl.BlockSpec(state_block, state_index, pipeline_mode=pl.Buffered(1)),
        ] + [_layer_spec(w.shape, l) for w, l in zip(weights, wlayer)],
        out_specs=[
            pl.BlockSpec((nb, tm, D_MODEL), lambda b, t: (b, t, 0)),
            pl.BlockSpec(state_block, state_index),
        ],
        out_shape=[
            jax.ShapeDtypeStruct((B, T, D_MODEL), F32),
            jax.ShapeDtypeStruct((1, B, GLA_HEADS, GLA_DK, GLA_DV), F32),
        ],
        scratch_shapes=[
            pltpu.VMEM((nb, tm, GLA_VAL_DIM), BF16),
        ],
        compiler_params=pltpu.CompilerParams(
            dimension_semantics=("arbitrary", "arbitrary"),
            vmem_limit_bytes=VMEM_LIMIT_BYTES),
        name="gla_layer",
    )(h, p, s0, *weights)


def _pack_rows(rows):
    out = [jnp.pad(r.astype(F32), (0, D_MODEL - r.shape[0])) for r in rows]
    out += [jnp.zeros((D_MODEL,), F32)] * (VEC_ROWS - len(out))
    return jnp.stack(out)


def kernel(x_prompt, x_sample, state_pool, state_gla, p_prompt, p_sample, norm_mix, norm_ffn,
           norm_ple, norm_final, w_pool, b_pool, pool_scale, w_gla_in, w_gla_gate_up, b_gla_gate,
           gla_norm, w_gla_out, w_ffn_gate, w_ffn_up, w_ffn_down, w_ple_proj, w_ple_gate):
    vec0 = _pack_rows([norm_mix[0], norm_ffn[0], norm_ple[0], b_pool[0], pool_scale[0]])[None]
    vec1 = _pack_rows([norm_mix[1], norm_ffn[1], norm_ple[1], b_gla_gate[0], gla_norm[0],
                       norm_final])[None]
    wpool = w_pool.astype(BF16)
    ffn = (w_ffn_gate.astype(BF16), w_ffn_up.astype(BF16), w_ffn_down.astype(BF16),
           w_ple_gate.astype(BF16), w_ple_proj.astype(BF16))
    rank_pad = GATE_RANK_PAD - GLA_GATE_RANK
    win = w_gla_in.astype(BF16)
    wgr = jnp.pad(w_gla_in[:, :, IN_R:].astype(BF16), ((0, 0), (0, 0), (0, rank_pad)))
    wgu = jnp.pad(w_gla_gate_up.astype(BF16), ((0, 0), (0, rank_pad), (0, 0)))
    wo = w_gla_out.astype(BF16)

    bp = x_prompt.shape[0]
    hist0 = jnp.zeros((1, bp, POOL_HIST, D_MODEL), F32)
    s00 = jnp.zeros((1, bp, GLA_HEADS, GLA_DK, GLA_DV), F32)

    def trunk(x, p, hist, s0, start, tile):
        h1, pool_state = _pool_layer(x, p, 0, hist, vec0, wpool, *ffn, start=start, **tile)
        y, gla_state = _gla_layer(h1, p, 1, s0, vec1, win, wgr, wgu, wo, *ffn, **tile)
        return y, pool_state, gla_state

    y_p, ps_p, gs_p = trunk(x_prompt, p_prompt, hist0, s00, 0,
                            dict(nb=1, tm=PROMPT_TILE, sub=PROMPT_SUB))
    bs, ts = x_sample.shape[:2]
    y_s, ps_s, gs_s = trunk(x_sample, p_sample, state_pool, state_gla, PAST_LEN,
                            dict(nb=bs, tm=ts, sub=ts))
    return (y_p, y_s, ps_p, ps_s, gs_p, gs_s)
```

```python
import functools

import jax
import jax.numpy as jnp
from jax import lax
from jax.experimental import pallas as pl
from jax.experimental.pallas import tpu as pltpu

D_MODEL = 1024
PAST_LEN = 2048
CHUNK = 64
POOL_WINDOWS = (2, 4, 8, 16)
POOL_GROUP = D_MODEL // len(POOL_WINDOWS)
POOL_HIST = max(POOL_WINDOWS) - 1
HIST_PAD = POOL_HIST + 1
GLA_HEADS = 4
GLA_KEY_DIM = D_MODEL // 2
GLA_VAL_DIM = D_MODEL
GLA_DK = GLA_KEY_DIM // GLA_HEADS
GLA_DV = GLA_VAL_DIM // GLA_HEADS
GLA_GATE_RANK = 16
GLA_GATE_NORMALIZER = 16.0
GATE_RANK_PAD = 128
IN_Q, IN_K, IN_V = 0, GLA_KEY_DIM, 2 * GLA_KEY_DIM
IN_G = IN_V + GLA_VAL_DIM
IN_R = IN_G + GLA_VAL_DIM
D_FF = 2816
PLE_DIM = 256
EPS = 1e-6

VMEM_LIMIT_BYTES = 58 * 1024 * 1024
FFN_CHUNK = 256
PROMPT_SUB = 256
POOL_PROMPT_TILE = 4 * PROMPT_SUB
GLA_PROMPT_TILE = 2 * PROMPT_SUB

BF16 = jnp.bfloat16
F32 = jnp.float32

V_NORM_MIX, V_NORM_FFN, V_NORM_PLE, V_A, V_B, V_C = 0, 1, 2, 3, 4, 5
VEC_ROWS = 8


def _dot(a, b):
    return jnp.dot(a, b, preferred_element_type=F32)


def _rms(x, g):
    ms = jnp.mean(x * x, axis=-1, keepdims=True)
    return x * lax.rsqrt(ms + EPS) * g


def _run(gen):
    for _ in gen:
        pass


def _interleave(a, b):
    total_a, total_b = next(a), next(b)
    done_a = done_b = 0.0
    live_a = live_b = True
    while live_a or live_b:
        pick_a = live_a and (not live_b or done_a / total_a <= done_b / total_b)
        try:
            if pick_a:
                done_a += next(a)
            else:
                done_b += next(b)
        except StopIteration:
            if pick_a:
                live_a = False
            else:
                live_b = False


def _pipeline(fronts, backs):
    _run(fronts[0])
    for i in range(1, len(fronts)):
        _interleave(backs[i - 1], fronts[i])
    _run(backs[-1])


class _Stream:
    def __init__(self, gen, start, span):
        self.gen = gen
        self.total = next(gen)
        self.start, self.span = start, span
        self.done = 0.0
        self.live = True

    def clock(self):
        return self.start + self.span * self.done / self.total

    def step(self):
        try:
            self.done += next(self.gen)
        except StopIteration:
            self.live = False


def _pipeline_staggered(fronts, backs):
    first = _Stream(fronts[0], 0.0, 0.0)
    first.span = first.total
    streams = [first]
    unit = None
    for i, back in enumerate(backs):
        sb = _Stream(back, 0.0, 0.0)
        unit = sb.total if unit is None else unit
        sb.start, sb.span = first.total + i * unit, 2.0 * unit
        streams.append(sb)
        if i + 1 < len(fronts):
            streams.append(_Stream(fronts[i + 1], first.total + i * unit, unit))
    while True:
        live = [s for s in streams if s.live]
        if not live:
            break
        min(live, key=_Stream.clock).step()


def _ffn_back(residual, p_ref, vec_ref, wg_ref, wu_ref, wd_ref, wpg_ref, wpp_ref, out_ref,
              *, nb, r0, sub, final_norm):
    ms = nb * sub
    yield 11950.0
    h = residual()
    yield 1024.0
    xn = _rms(h, vec_ref[V_NORM_FFN:V_NORM_FFN + 1, :]).astype(BF16)
    yield 400.0
    acc = None
    cols = list(range(0, D_FF, FFN_CHUNK))
    nxt = (_dot(xn, wg_ref[:, 0:FFN_CHUNK]), _dot(xn, wu_ref[:, 0:FFN_CHUNK]))
    for i, c0 in enumerate(cols):
        gate, up = nxt
        if i + 1 < len(cols):
            c1 = cols[i + 1]
            nxt = (_dot(xn, wg_ref[:, c1:c1 + FFN_CHUNK]), _dot(xn, wu_ref[:, c1:c1 + FFN_CHUNK]))
        act = (gate * jax.nn.sigmoid(gate) * up).astype(BF16)
        part = _dot(act, wd_ref[c0:c0 + FFN_CHUNK, :])
        acc = part if acc is None else acc + part
        yield 768.0
    h = h + acc
    xn = _rms(h, vec_ref[V_NORM_PLE:V_NORM_PLE + 1, :]).astype(BF16)
    yield 400.0
    ple_gate = jax.nn.sigmoid(_dot(xn, wpg_ref[...]))
    yield 1024.0
    p_bf = p_ref[:, r0:r0 + sub, :].reshape(ms, PLE_DIM).astype(BF16)
    h = h + ple_gate * _dot(p_bf, wpp_ref[...])
    yield 256.0
    if final_norm:
        h = _rms(h, vec_ref[V_C:V_C + 1, :])
    out_ref[:, r0:r0 + sub, :] = h.reshape(nb, sub, D_MODEL)
    yield 400.0


def _pool_front(x_ref, vec_ref, wpool_ref, buf_ref, d_ref, carry, *, nb, r0, sub, pos0):
    ms = nb * sub
    yield 2400.0
    x = x_ref[:, r0:r0 + sub, :].reshape(ms, D_MODEL)
    xn = _rms(x, vec_ref[V_NORM_MIX:V_NORM_MIX + 1, :])
    buf_ref[:, HIST_PAD + r0:HIST_PAD + r0 + sub, :] = xn.reshape(nb, sub, D_MODEL)
    yield 600.0
    pos = pos0 + r0 + lax.broadcasted_iota(jnp.int32, (sub, 1), 0)
    d0 = r0 * nb
    ys = []
    for gi, w in enumerate(POOL_WINDOWS):
        sl = slice(gi * POOL_GROUP, (gi + 1) * POOL_GROUP)
        inv_cnt = 1.0 / jnp.minimum(w, pos + 1).astype(F32)
        for bi in range(nb):
            ext = buf_ref[bi, r0:r0 + HIST_PAD + sub, sl]
            acc = ext
            span = 1
            while span < w:
                acc = acc + pltpu.roll(acc, span, 0)
                span *= 2
            cur = ext[HIST_PAD:, :]
            d = acc[HIST_PAD:, :] * inv_cnt - cur
            d_ref[d0 + bi * sub:d0 + (bi + 1) * sub, sl] = d.astype(BF16)
        ys.append(_dot(d_ref[d0:d0 + ms, sl], wpool_ref[gi]))
        yield 100.0 * (gi + 2)
    y = jnp.concatenate(ys, axis=1)
    y = (y + vec_ref[V_A:V_A + 1, :]) * vec_ref[V_B:V_B + 1, :]
    carry[r0] = x + y
    yield 400.0


def _pool_layer_kernel(x_ref, p_ref, hist_ref, vec_ref, wpool_ref, wg_ref, wu_ref, wd_ref,
                       wpg_ref, wpp_ref, h_out_ref, state_out_ref, buf_ref, d_ref,
                       *, nb, tm, sub, start):
    t = pl.program_id(1)

    @pl.when(t == 0)
    def _():
        buf_ref[:, 0:1, :] = jnp.zeros((nb, 1, D_MODEL), F32)
        buf_ref[:, 1:HIST_PAD, :] = hist_ref[...]

    carry = {}
    starts = list(range(0, tm, sub))
    fronts = [_pool_front(x_ref, vec_ref, wpool_ref, buf_ref, d_ref, carry,
                          nb=nb, r0=r0, sub=sub, pos0=start + t * tm) for r0 in starts]
    backs = [_ffn_back(functools.partial(carry.get, r0), p_ref, vec_ref, wg_ref, wu_ref, wd_ref,
                       wpg_ref, wpp_ref, h_out_ref, nb=nb, r0=r0, sub=sub, final_norm=False)
             for r0 in starts]
    _pipeline_staggered(fronts, backs)

    @pl.when(t == pl.num_programs(1) - 1)
    def _():
        state_out_ref[...] = buf_ref[:, tm + 1:tm + HIST_PAD, :]

    buf_ref[:, 0:HIST_PAD, :] = buf_ref[:, tm:tm + HIST_PAD, :]


def _layer_spec(shape, layer):
    nd = len(shape) - 1
    return pl.BlockSpec((None,) + tuple(shape[1:]), lambda b, t: (layer,) + (0,) * nd,
                        pipeline_mode=pl.Buffered(1))


def _pool_layer(x, p, layer, hist, vecs, wpool, wg, wu, wd, wpg, wpp, *, nb, tm, sub, start):
    weights = (vecs, wpool, wg, wu, wd, wpg, wpp)
    wlayer = (0, 0, layer, layer, layer, layer, layer)
    B, T, _ = x.shape
    grid = (B // nb, T // tm)
    kern = functools.partial(_pool_layer_kernel, nb=nb, tm=tm, sub=sub, start=start)
    state_spec = pl.BlockSpec((None, nb, POOL_HIST, D_MODEL), lambda b, t: (0, b, 0, 0))
    return pl.pallas_call(
        kern,
        grid=grid,
        in_specs=[
            pl.BlockSpec((nb, tm, D_MODEL), lambda b, t: (b, t, 0)),
            pl.BlockSpec((None, nb, tm, PLE_DIM), lambda b, t: (layer, b, t, 0)),
            state_spec,
        ] + [_layer_spec(w.shape, l) for w, l in zip(weights, wlayer)],
        out_specs=[
            pl.BlockSpec((nb, tm, D_MODEL), lambda b, t: (b, t, 0)),
            state_spec,
        ],
        out_shape=[
            jax.ShapeDtypeStruct((B, T, D_MODEL), F32),
            jax.ShapeDtypeStruct((1, B, POOL_HIST, D_MODEL), F32),
        ],
        scratch_shapes=[
            pltpu.VMEM((nb, HIST_PAD + tm, D_MODEL), F32),
            pltpu.VMEM((nb * tm, D_MODEL), BF16),
        ],
        compiler_params=pltpu.CompilerParams(
            dimension_semantics=("arbitrary", "arbitrary"),
            vmem_limit_bytes=VMEM_LIMIT_BYTES),
        name="pool_layer",
    )(x, p, hist, *weights)


def _gla_front(vec_ref, win_ref, wgr_ref, wgu_ref, s_ref, o_s, h_ref, state,
               *, nb, src_r0, r0, sub, chunk):
    ms = nb * sub
    segs_per_batch = sub // chunk
    nseg = nb * segs_per_batch
    half = GLA_VAL_DIM // 2
    yield 9700.0
    h = h_ref[:, src_r0:src_r0 + sub, :].reshape(ms, D_MODEL)
    xn = _rms(h, vec_ref[V_NORM_MIX:V_NORM_MIX + 1, :]).astype(BF16)
    yield 400.0
    gr = _dot(xn, wgr_ref[...]).astype(BF16)
    q = _dot(xn, win_ref[:, IN_Q:IN_K]) * (GLA_DK ** -0.5)
    yield 512.0
    z = _dot(gr, wgu_ref[...]) + vec_ref[V_A:V_A + 1, 0:GLA_KEY_DIM]
    log_sig = jnp.minimum(z, 0.0) - jnp.log1p(jnp.exp(-jnp.abs(z)))
    la = log_sig * (1.0 / GLA_GATE_NORMALIZER)
    la_hi = la.astype(BF16)
    la_lo = (la - la_hi.astype(F32)).astype(BF16)
    k = _dot(xn, win_ref[:, IN_K:IN_V])
    yield 512.0
    row = lax.broadcasted_iota(jnp.int32, (ms, ms), 0)
    col = lax.broadcasted_iota(jnp.int32, (ms, ms), 1)
    causal = ((row & (-chunk)) == (col & (-chunk))) & (col <= row)
    tri = causal.astype(BF16)
    b = _dot(tri, la_hi) + _dot(tri, la_lo)
    v_parts = []
    for c0 in (IN_V, IN_V + half):
        v_parts.append(_dot(xn, win_ref[:, c0:c0 + half]).astype(BF16))
        yield 512.0
    v = jnp.concatenate(v_parts, axis=1)
    b_last = [b[(s + 1) * chunk - 1:(s + 1) * chunk, :] for s in range(nseg)]
    b_last_rows = jnp.concatenate(
        [jnp.broadcast_to(bl, (chunk, GLA_KEY_DIM)) for bl in b_last], axis=0)
    q_dec = (q * jnp.exp(b)).astype(BF16)
    k_inv = (k * jnp.exp(-b)).astype(BF16)
    k_end = (k * jnp.exp(b_last_rows - b)).astype(BF16)
    e_last = [jnp.exp(bl) for bl in b_last]
    yield 300.0
    g_parts = []
    for c0 in (IN_G, IN_G + half):
        g = _dot(xn, win_ref[:, c0:c0 + half])
        g_parts.append(g * jax.nn.sigmoid(g))
        yield 512.0
    out_gate = jnp.concatenate(g_parts, axis=1)

    eye = (lax.broadcasted_iota(jnp.int32, (GLA_DK, GLA_DK), 0)
           == lax.broadcasted_iota(jnp.int32, (GLA_DK, GLA_DK), 1)).astype(F32)
    gnorm = vec_ref[V_B:V_B + 1, 0:GLA_DV]
    row_seg = lax.broadcasted_iota(jnp.int32, (ms, GLA_DK), 0) & (-chunk)

    for hh in range(GLA_HEADS):
        ks = slice(hh * GLA_DK, (hh + 1) * GLA_DK)
        vs = slice(hh * GLA_DV, (hh + 1) * GLA_DV)
        qd_h = q_dec[:, ks]
        v_h = v[:, vs]
        scores = lax.dot_general(qd_h, k_inv[:, ks], (((1,), (1,)), ((), ())),
                                 preferred_element_type=F32)
        scores = jnp.where(causal, scores, 0.0).astype(BF16)
        o_intra = _dot(scores, v_h)
        ke_h = k_end[:, ks]
        ke_blk = jnp.concatenate(
            [jnp.where(row_seg == s * chunk, ke_h, jnp.zeros_like(ke_h)) for s in range(nseg)],
            axis=1)
        kv_all = lax.dot_general(ke_blk, v_h, (((0,), (0,)), ((), ())), preferred_element_type=F32)
        yield 700.0
        o_inter = []
        for bi in range(nb):
            if (bi, hh) not in state:
                state[bi, hh] = s_ref[bi, hh]
            for ci in range(segs_per_batch):
                s = bi * segs_per_batch + ci
                rows = slice(s * chunk, (s + 1) * chunk)
                o_inter.append(_dot(qd_h[rows, :], state[bi, hh].astype(BF16)))
                e_col = jnp.sum(eye * e_last[s][:, ks], axis=1, keepdims=True)
                state[bi, hh] = e_col * state[bi, hh] + kv_all[s * GLA_DK:(s + 1) * GLA_DK, :]
        o = o_intra + jnp.concatenate(o_inter, axis=0)
        o = o * lax.rsqrt(jnp.mean(o * o, axis=-1, keepdims=True) + EPS) * gnorm
        o_s[:, r0:r0 + sub, vs] = (o * out_gate[:, vs]).astype(BF16).reshape(nb, sub, GLA_DV)
        yield 700.0


def _gla_residual(h_ref, o_s, wo_ref, *, nb, r0, sub):
    ms = nb * sub
    h = h_ref[:, r0:r0 + sub, :].reshape(ms, D_MODEL)
    return h + _dot(o_s[:, r0:r0 + sub, :].reshape(ms, GLA_VAL_DIM), wo_ref[...])


def _gla_layer_kernel(h_ref, hnext_ref, p_ref, s0_ref, vec_ref, win_ref, wgr_ref, wgu_ref, wo_ref,
                      wg_ref, wu_ref, wd_ref, wpg_ref, wpp_ref,
                      y_out_ref, s_ref, o_s, *, nb, tm, sub, chunk, prefetch):
    t = pl.program_id(1)
    starts = list(range(0, tm, sub))
    front = functools.partial(_gla_front, vec_ref, win_ref, wgr_ref, wgu_ref, s_ref, o_s,
                              nb=nb, sub=sub, chunk=chunk)
    backs = [_ffn_back(functools.partial(_gla_residual, h_ref, o_s, wo_ref, nb=nb, r0=r0, sub=sub),
                       p_ref, vec_ref, wg_ref, wu_ref, wd_ref, wpg_ref, wpp_ref, y_out_ref,
                       nb=nb, r0=r0, sub=sub, final_norm=True) for r0 in starts]

    if not prefetch:
        @pl.when(t == 0)
        def _():
            s_ref[...] = s0_ref[...]

        state = {}
        _pipeline([front(h_ref, state, src_r0=r0, r0=r0) for r0 in starts], backs)
        for key, s in state.items():
            s_ref[key] = s
        return

    @pl.when(t == 0)
    def _():
        s_ref[...] = s0_ref[...]
        first = {}
        _run(front(h_ref, first, src_r0=0, r0=0))
        for key, s in first.items():
            s_ref[key] = s

    state = {}
    for i, r0 in enumerate(starts[1:]):
        _interleave(backs[i], front(h_ref, state, src_r0=r0, r0=r0))
    done = dict(state)
    _interleave(backs[-1], front(hnext_ref, state, src_r0=0, r0=0))
    has_next = t + 1 < pl.num_programs(1)
    for key, s in state.items():
        s_ref[key] = jnp.where(has_next, s, done[key])


def _gla_layer(h, p, layer, s0, vecs, win, wgr, wgu, wo, wg, wu, wd, wpg, wpp, *, nb, tm, sub):
    B, T, _ = h.shape
    chunk = min(CHUNK, T)
    grid = (B // nb, T // tm)
    prefetch = grid[1] > 1 and tm > sub
    kern = functools.partial(_gla_layer_kernel, nb=nb, tm=tm, sub=sub, chunk=chunk,
                             prefetch=prefetch)
    subs_per_tile, last_sub = tm // sub, T // sub - 1
    weights = (vecs, win, wgr, wgu, wo, wg, wu, wd, wpg, wpp)
    wlayer = (0, 0, 0, 0, 0, layer, layer, layer, layer, layer)
    state_block = (None, nb, GLA_HEADS, GLA_DK, GLA_DV)

    def state_index(b, t):
        return (0, b, 0, 0, 0)

    return pl.pallas_call(
        kern,
        grid=grid,
        in_specs=[
            pl.BlockSpec((nb, tm, D_MODEL), lambda b, t: (b, t, 0)),
            pl.BlockSpec((nb, sub, D_MODEL),
                         lambda b, t: (b, jnp.minimum((t + 1) * subs_per_tile, last_sub), 0)),
            pl.BlockSpec((None, nb, tm, PLE_DIM), lambda b, t: (layer, b, t, 0)),
            pl.BlockSpec(state_block, state_index, pipeline_mode=pl.Buffered(1)),
        ] + [_layer_spec(w.shape, l) for w, l in zip(weights, wlayer)],
        out_specs=[
            pl.BlockSpec((nb, tm, D_MODEL), lambda b, t: (b, t, 0)),
            pl.BlockSpec(state_block, state_index),
        ],
        out_shape=[
            jax.ShapeDtypeStruct((B, T, D_MODEL), F32),
            jax.ShapeDtypeStruct((1, B, GLA_HEADS, GLA_DK, GLA_DV), F32),
        ],
        scratch_shapes=[
            pltpu.VMEM((nb, tm, GLA_VAL_DIM), BF16),
        ],
        compiler_params=pltpu.CompilerParams(
            dimension_semantics=("arbitrary", "arbitrary"),
            vmem_limit_bytes=VMEM_LIMIT_BYTES),
        name="gla_layer",
    )(h, h, p, s0, *weights)


def _pack_rows(rows):
    out = [jnp.pad(r.astype(F32), (0, D_MODEL - r.shape[0])) for r in rows]
    out += [jnp.zeros((D_MODEL,), F32)] * (VEC_ROWS - len(out))
    return jnp.stack(out)


def kernel(x_prompt, x_sample, state_pool, state_gla, p_prompt, p_sample, norm_mix, norm_ffn,
           norm_ple, norm_final, w_pool, b_pool, pool_scale, w_gla_in, w_gla_gate_up, b_gla_gate,
           gla_norm, w_gla_out, w_ffn_gate, w_ffn_up, w_ffn_down, w_ple_proj, w_ple_gate):
    vec0 = _pack_rows([norm_mix[0], norm_ffn[0], norm_ple[0], b_pool[0], pool_scale[0]])[None]
    vec1 = _pack_rows([norm_mix[1], norm_ffn[1], norm_ple[1], b_gla_gate[0], gla_norm[0],
                       norm_final])[None]
    wpool = w_pool.astype(BF16)
    ffn = (w_ffn_gate.astype(BF16), w_ffn_up.astype(BF16), w_ffn_down.astype(BF16),
           w_ple_gate.astype(BF16), w_ple_proj.astype(BF16))
    rank_pad = GATE_RANK_PAD - GLA_GATE_RANK
    win = w_gla_in.astype(BF16)
    wgr = jnp.pad(w_gla_in[:, :, IN_R:].astype(BF16), ((0, 0), (0, 0), (0, rank_pad)))
    wgu = jnp.pad(w_gla_gate_up.astype(BF16), ((0, 0), (0, rank_pad), (0, 0)))
    wo = w_gla_out.astype(BF16)

    bp = x_prompt.shape[0]
    hist0 = jnp.zeros((1, bp, POOL_HIST, D_MODEL), F32)
    s00 = jnp.zeros((1, bp, GLA_HEADS, GLA_DK, GLA_DV), F32)

    def trunk(x, p, hist, s0, start, pool_tile, gla_tile):
        h1, pool_state = _pool_layer(x, p, 0, hist, vec0, wpool, *ffn, start=start, **pool_tile)
        y, gla_state = _gla_layer(h1, p, 1, s0, vec1, win, wgr, wgu, wo, *ffn, **gla_tile)
        return y, pool_state, gla_state

    y_p, ps_p, gs_p = trunk(x_prompt, p_prompt, hist0, s00, 0,
                            dict(nb=1, tm=POOL_PROMPT_TILE, sub=PROMPT_SUB),
                            dict(nb=1, tm=GLA_PROMPT_TILE, sub=PROMPT_SUB))
    bs, ts = x_sample.shape[:2]
    sample_tile = dict(nb=bs, tm=ts, sub=ts)
    y_s, ps_s, gs_s = trunk(x_sample, p_sample, state_pool, state_gla, PAST_LEN,
                            sample_tile, sample_tile)
    return (y_p, y_s, ps_p, ps_s, gs_p, gs_s)
```

```python
import functools

import jax
import jax.numpy as jnp
from jax import lax
from jax.experimental import pallas as pl
from jax.experimental.pallas import tpu as pltpu

D_MODEL = 1024
PAST_LEN = 2048
CHUNK = 64
POOL_WINDOWS = (2, 4, 8, 16)
POOL_GROUP = D_MODEL // len(POOL_WINDOWS)
POOL_HIST = max(POOL_WINDOWS) - 1
HIST_PAD = POOL_HIST + 1
GLA_HEADS = 4
GLA_KEY_DIM = D_MODEL // 2
GLA_VAL_DIM = D_MODEL
GLA_DK = GLA_KEY_DIM // GLA_HEADS
GLA_DV = GLA_VAL_DIM // GLA_HEADS
GLA_GATE_RANK = 16
GLA_GATE_NORMALIZER = 16.0
GATE_RANK_PAD = 128
IN_Q, IN_K, IN_V = 0, GLA_KEY_DIM, 2 * GLA_KEY_DIM
IN_G = IN_V + GLA_VAL_DIM
IN_R = IN_G + GLA_VAL_DIM
D_FF = 2816
PLE_DIM = 256
EPS = 1e-6

VMEM_LIMIT_BYTES = 58 * 1024 * 1024
FFN_CHUNK = 256
PROMPT_SUB = 256
POOL_PROMPT_TILE = 4 * PROMPT_SUB
GLA_PROMPT_TILE = 2 * PROMPT_SUB

BF16 = jnp.bfloat16
F32 = jnp.float32

V_NORM_MIX, V_NORM_FFN, V_NORM_PLE, V_A, V_B, V_C = 0, 1, 2, 3, 4, 5
VEC_ROWS = 8


def _dot(a, b):
    return jnp.dot(a, b, preferred_element_type=F32)


def _rms(x, g):
    ms = jnp.mean(x * x, axis=-1, keepdims=True)
    return x * lax.rsqrt(ms + EPS) * g


def _run(gen):
    for _ in gen:
        pass


def _interleave(a, b):
    total_a, total_b = next(a), next(b)
    done_a = done_b = 0.0
    live_a = live_b = True
    while live_a or live_b:
        pick_a = live_a and (not live_b or done_a / total_a <= done_b / total_b)
        try:
            if pick_a:
                done_a += next(a)
            else:
                done_b += next(b)
        except StopIteration:
            if pick_a:
                live_a = False
            else:
                live_b = False


def _pipeline(fronts, backs):
    _run(fronts[0])
    for i in range(1, len(fronts)):
        _interleave(backs[i - 1], fronts[i])
    _run(backs[-1])


class _Stream:
    def __init__(self, gen, start, span):
        self.gen = gen
        self.total = next(gen)
        self.start, self.span = start, span
        self.done = 0.0
        self.live = True

    def clock(self):
        return self.start + self.span * self.done / self.total

    def step(self):
        try:
            self.done += next(self.gen)
        except StopIteration:
            self.live = False


def _pipeline_staggered(fronts, backs):
    first = _Stream(fronts[0], 0.0, 0.0)
    first.span = first.total
    streams = [first]
    unit = None
    for i, back in enumerate(backs):
        sb = _Stream(back, 0.0, 0.0)
        unit = sb.total if unit is None else unit
        sb.start, sb.span = first.total + i * unit, 2.0 * unit
        streams.append(sb)
        if i + 1 < len(fronts):
            streams.append(_Stream(fronts[i + 1], first.total + i * unit, unit))
    while True:
        live = [s for s in streams if s.live]
        if not live:
            break
        min(live, key=_Stream.clock).step()


def _ffn_back(residual, p_ref, vec_ref, wg_ref, wu_ref, wd_ref, wpg_ref, wpp_ref, out_ref,
              *, nb, r0, sub, final_norm):
    ms = nb * sub
    yield 11950.0
    h = residual()
    yield 1024.0
    xn = _rms(h, vec_ref[V_NORM_FFN:V_NORM_FFN + 1, :]).astype(BF16)
    yield 400.0
    acc = None
    cols = list(range(0, D_FF, FFN_CHUNK))
    nxt = (_dot(xn, wg_ref[:, 0:FFN_CHUNK]), _dot(xn, wu_ref[:, 0:FFN_CHUNK]))
    for i, c0 in enumerate(cols):
        gate, up = nxt
        if i + 1 < len(cols):
            c1 = cols[i + 1]
            nxt = (_dot(xn, wg_ref[:, c1:c1 + FFN_CHUNK]), _dot(xn, wu_ref[:, c1:c1 + FFN_CHUNK]))
        act = (gate * jax.nn.sigmoid(gate) * up).astype(BF16)
        part = _dot(act, wd_ref[c0:c0 + FFN_CHUNK, :])
        acc = part if acc is None else acc + part
        yield 768.0
    h = h + acc
    xn = _rms(h, vec_ref[V_NORM_PLE:V_NORM_PLE + 1, :]).astype(BF16)
    yield 400.0
    ple_gate = jax.nn.sigmoid(_dot(xn, wpg_ref[...]))
    yield 1024.0
    p_bf = p_ref[:, r0:r0 + sub, :].reshape(ms, PLE_DIM).astype(BF16)
    h = h + ple_gate * _dot(p_bf, wpp_ref[...])
    yield 256.0
    if final_norm:
        h = _rms(h, vec_ref[V_C:V_C + 1, :])
    out_ref[:, r0:r0 + sub, :] = h.reshape(nb, sub, D_MODEL)
    yield 400.0


def _pool_front(x_ref, vec_ref, wpool_ref, buf_ref, d_ref, carry, *, nb, r0, sub, pos0):
    ms = nb * sub
    yield 2400.0
    x = x_ref[:, r0:r0 + sub, :].reshape(ms, D_MODEL)
    xn = _rms(x, vec_ref[V_NORM_MIX:V_NORM_MIX + 1, :])
    buf_ref[:, HIST_PAD + r0:HIST_PAD + r0 + sub, :] = xn.reshape(nb, sub, D_MODEL)
    yield 600.0
    pos = pos0 + r0 + lax.broadcasted_iota(jnp.int32, (sub, 1), 0)
    d0 = r0 * nb
    ys = []
    for gi, w in enumerate(POOL_WINDOWS):
        sl = slice(gi * POOL_GROUP, (gi + 1) * POOL_GROUP)
        inv_cnt = 1.0 / jnp.minimum(w, pos + 1).astype(F32)
        for bi in range(nb):
            ext = buf_ref[bi, r0:r0 + HIST_PAD + sub, sl]
            acc = ext
            span = 1
            while span < w:
                acc = acc + pltpu.roll(acc, span, 0)
                span *= 2
            cur = ext[HIST_PAD:, :]
            d = acc[HIST_PAD:, :] * inv_cnt - cur
            d_ref[d0 + bi * sub:d0 + (bi + 1) * sub, sl] = d.astype(BF16)
        ys.append(_dot(d_ref[d0:d0 + ms, sl], wpool_ref[gi]))
        yield 100.0 * (gi + 2)
    y = jnp.concatenate(ys, axis=1)
    y = (y + vec_ref[V_A:V_A + 1, :]) * vec_ref[V_B:V_B + 1, :]
    carry[r0] = x + y
    yield 400.0


def _pool_layer_kernel(x_ref, p_ref, hist_ref, vec_ref, wpool_ref, wg_ref, wu_ref, wd_ref,
                       wpg_ref, wpp_ref, h_out_ref, state_out_ref, buf_ref, d_ref,
                       *, nb, tm, sub, start):
    t = pl.program_id(1)

    @pl.when(t == 0)
    def _():
        buf_ref[:, 0:1, :] = jnp.zeros((nb, 1, D_MODEL), F32)
        buf_ref[:, 1:HIST_PAD, :] = hist_ref[...]

    carry = {}
    starts = list(range(0, tm, sub))
    fronts = [_pool_front(x_ref, vec_ref, wpool_ref, buf_ref, d_ref, carry,
                          nb=nb, r0=r0, sub=sub, pos0=start + t * tm) for r0 in starts]
    backs = [_ffn_back(functools.partial(carry.get, r0), p_ref, vec_ref, wg_ref, wu_ref, wd_ref,
                       wpg_ref, wpp_ref, h_out_ref, nb=nb, r0=r0, sub=sub, final_norm=False)
             for r0 in starts]
    _pipeline_staggered(fronts, backs)

    @pl.when(t == pl.num_programs(1) - 1)
    def _():
        state_out_ref[...] = buf_ref[:, tm + 1:tm + HIST_PAD, :]

    buf_ref[:, 0:HIST_PAD, :] = buf_ref[:, tm:tm + HIST_PAD, :]


def _layer_spec(shape, layer):
    nd = len(shape) - 1
    return pl.BlockSpec((None,) + tuple(shape[1:]), lambda b, t: (layer,) + (0,) * nd,
                        pipeline_mode=pl.Buffered(1))


def _pool_layer(x, p, layer, hist, vecs, wpool, wg, wu, wd, wpg, wpp, *, nb, tm, sub, start):
    weights = (vecs, wpool, wg, wu, wd, wpg, wpp)
    wlayer = (0, 0, layer, layer, layer, layer, layer)
    B, T, _ = x.shape
    grid = (B // nb, T // tm)
    kern = functools.partial(_pool_layer_kernel, nb=nb, tm=tm, sub=sub, start=start)
    state_spec = pl.BlockSpec((None, nb, POOL_HIST, D_MODEL), lambda b, t: (0, b, 0, 0))
    return pl.pallas_call(
        kern,
        grid=grid,
        in_specs=[
            pl.BlockSpec((nb, tm, D_MODEL), lambda b, t: (b, t, 0)),
            pl.BlockSpec((None, nb, tm, PLE_DIM), lambda b, t: (layer, b, t, 0)),
            state_spec,
        ] + [_layer_spec(w.shape, l) for w, l in zip(weights, wlayer)],
        out_specs=[
            pl.BlockSpec((nb, tm, D_MODEL), lambda b, t: (b, t, 0)),
            state_spec,
        ],
        out_shape=[
            jax.ShapeDtypeStruct((B, T, D_MODEL), F32),
            jax.ShapeDtypeStruct((1, B, POOL_HIST, D_MODEL), F32),
        ],
        scratch_shapes=[
            pltpu.VMEM((nb, HIST_PAD + tm, D_MODEL), F32),
            pltpu.VMEM((nb * tm, D_MODEL), BF16),
        ],
        compiler_params=pltpu.CompilerParams(
            dimension_semantics=("arbitrary", "arbitrary"),
            vmem_limit_bytes=VMEM_LIMIT_BYTES),
        name="pool_layer",
    )(x, p, hist, *weights)


def _gla_front(h_ref, vec_ref, win_ref, wgr_ref, wgu_ref, s_ref, o_s, state,
               *, nb, r0, sub, chunk):
    ms = nb * sub
    segs_per_batch = sub // chunk
    nseg = nb * segs_per_batch
    half = GLA_VAL_DIM // 2
    yield 9700.0
    h = h_ref[:, r0:r0 + sub, :].reshape(ms, D_MODEL)
    xn = _rms(h, vec_ref[V_NORM_MIX:V_NORM_MIX + 1, :]).astype(BF16)
    yield 400.0
    gr = _dot(xn, wgr_ref[...]).astype(BF16)
    q = _dot(xn, win_ref[:, IN_Q:IN_K]) * (GLA_DK ** -0.5)
    yield 512.0
    z = _dot(gr, wgu_ref[...]) + vec_ref[V_A:V_A + 1, 0:GLA_KEY_DIM]
    log_sig = jnp.minimum(z, 0.0) - jnp.log1p(jnp.exp(-jnp.abs(z)))
    la = log_sig * (1.0 / GLA_GATE_NORMALIZER)
    la_hi = la.astype(BF16)
    la_lo = (la - la_hi.astype(F32)).astype(BF16)
    k = _dot(xn, win_ref[:, IN_K:IN_V])
    yield 512.0
    row = lax.broadcasted_iota(jnp.int32, (ms, ms), 0)
    col = lax.broadcasted_iota(jnp.int32, (ms, ms), 1)
    causal = ((row & (-chunk)) == (col & (-chunk))) & (col <= row)
    tri = causal.astype(BF16)
    b = _dot(tri, la_hi) + _dot(tri, la_lo)
    v_parts = []
    for c0 in (IN_V, IN_V + half):
        v_parts.append(_dot(xn, win_ref[:, c0:c0 + half]).astype(BF16))
        yield 512.0
    v = jnp.concatenate(v_parts, axis=1)
    b_last = [b[(s + 1) * chunk - 1:(s + 1) * chunk, :] for s in range(nseg)]
    b_last_rows = jnp.concatenate(
        [jnp.broadcast_to(bl, (chunk, GLA_KEY_DIM)) for bl in b_last], axis=0)
    q_dec = (q * jnp.exp(b)).astype(BF16)
    k_inv = (k * jnp.exp(-b)).astype(BF16)
    k_end = (k * jnp.exp(b_last_rows - b)).astype(BF16)
    e_last = [jnp.exp(bl) for bl in b_last]
    yield 300.0
    g_parts = []
    for c0 in (IN_G, IN_G + half):
        g = _dot(xn, win_ref[:, c0:c0 + half])
        g_parts.append(g * jax.nn.sigmoid(g))
        yield 512.0
    out_gate = jnp.concatenate(g_parts, axis=1)

    eye = (lax.broadcasted_iota(jnp.int32, (GLA_DK, GLA_DK), 0)
           == lax.broadcasted_iota(jnp.int32, (GLA_DK, GLA_DK), 1)).astype(F32)
    gnorm = vec_ref[V_B:V_B + 1, 0:GLA_DV]
    row_seg = lax.broadcasted_iota(jnp.int32, (ms, GLA_DK), 0) & (-chunk)

    for hh in range(GLA_HEADS):
        ks = slice(hh * GLA_DK, (hh + 1) * GLA_DK)
        vs = slice(hh * GLA_DV, (hh + 1) * GLA_DV)
        qd_h = q_dec[:, ks]
        v_h = v[:, vs]
        scores = lax.dot_general(qd_h, k_inv[:, ks], (((1,), (1,)), ((), ())),
                                 preferred_element_type=F32)
        scores = jnp.where(causal, scores, 0.0).astype(BF16)
        o_intra = _dot(scores, v_h)
        ke_h = k_end[:, ks]
        ke_blk = jnp.concatenate(
            [jnp.where(row_seg == s * chunk, ke_h, jnp.zeros_like(ke_h)) for s in range(nseg)],
            axis=1)
        kv_all = lax.dot_general(ke_blk, v_h, (((0,), (0,)), ((), ())), preferred_element_type=F32)
        yield 700.0
        o_inter = []
        for bi in range(nb):
            if (bi, hh) not in state:
                state[bi, hh] = s_ref[bi, hh]
            for ci in range(segs_per_batch):
                s = bi * segs_per_batch + ci
                rows = slice(s * chunk, (s + 1) * chunk)
                o_inter.append(_dot(qd_h[rows, :], state[bi, hh].astype(BF16)))
                e_col = jnp.sum(eye * e_last[s][:, ks], axis=1, keepdims=True)
                state[bi, hh] = e_col * state[bi, hh] + kv_all[s * GLA_DK:(s + 1) * GLA_DK, :]
        o = o_intra + jnp.concatenate(o_inter, axis=0)
        o = o * lax.rsqrt(jnp.mean(o * o, axis=-1, keepdims=True) + EPS) * gnorm
        o_s[:, r0:r0 + sub, vs] = (o * out_gate[:, vs]).astype(BF16).reshape(nb, sub, GLA_DV)
        yield 700.0


def _gla_residual(h_ref, o_s, wo_ref, *, nb, r0, sub):
    ms = nb * sub
    h = h_ref[:, r0:r0 + sub, :].reshape(ms, D_MODEL)
    return h + _dot(o_s[:, r0:r0 + sub, :].reshape(ms, GLA_VAL_DIM), wo_ref[...])


def _gla_layer_kernel(h_ref, p_ref, s0_ref, vec_ref, win_ref, wgr_ref, wgu_ref, wo_ref,
                      wg_ref, wu_ref, wd_ref, wpg_ref, wpp_ref,
                      y_out_ref, s_ref, o_s, *, nb, tm, sub, chunk):
    t = pl.program_id(1)

    @pl.when(t == 0)
    def _():
        s_ref[...] = s0_ref[...]

    state = {}
    starts = list(range(0, tm, sub))
    fronts = [_gla_front(h_ref, vec_ref, win_ref, wgr_ref, wgu_ref, s_ref, o_s, state,
                         nb=nb, r0=r0, sub=sub, chunk=chunk) for r0 in starts]
    backs = [_ffn_back(functools.partial(_gla_residual, h_ref, o_s, wo_ref, nb=nb, r0=r0, sub=sub),
                       p_ref, vec_ref, wg_ref, wu_ref, wd_ref, wpg_ref, wpp_ref, y_out_ref,
                       nb=nb, r0=r0, sub=sub, final_norm=True) for r0 in starts]
    _pipeline(fronts, backs)
    for (bi, hh), s in state.items():
        s_ref[bi, hh] = s


def _gla_layer(h, p, layer, s0, vecs, win, wgr, wgu, wo, wg, wu, wd, wpg, wpp, *, nb, tm, sub):
    B, T, _ = h.shape
    chunk = min(CHUNK, T)
    grid = (B // nb, T // tm)
    kern = functools.partial(_gla_layer_kernel, nb=nb, tm=tm, sub=sub, chunk=chunk)
    weights = (vecs, win, wgr, wgu, wo, wg, wu, wd, wpg, wpp)
    wlayer = (0, 0, 0, 0, 0, layer, layer, layer, layer, layer)
    state_block = (None, nb, GLA_HEADS, GLA_DK, GLA_DV)

    def state_index(b, t):
        return (0, b, 0, 0, 0)

    return pl.pallas_call(
        kern,
        grid=grid,
        in_specs=[
            pl.BlockSpec((nb, tm, D_MODEL), lambda b, t: (b, t, 0)),
            pl.BlockSpec((None, nb, tm, PLE_DIM), lambda b, t: (layer, b, t, 0)),
            pl.BlockSpec(state_block, state_index, pipeline_mode=pl.Buffered(1)),
        ] + [_layer_spec(w.shape, l) for w, l in zip(weights, wlayer)],
        out_specs=[
            pl.BlockSpec((nb, tm, D_MODEL), lambda b, t: (b, t, 0)),
            pl.BlockSpec(state_block, state_index),
        ],
        out_shape=[
            jax.ShapeDtypeStruct((B, T, D_MODEL), F32),
            jax.ShapeDtypeStruct((1, B, GLA_HEADS, GLA_DK, GLA_DV), F32),
        ],
        scratch_shapes=[
            pltpu.VMEM((nb, tm, GLA_VAL_DIM), BF16),
        ],
        compiler_params=pltpu.CompilerParams(
            dimension_semantics=("arbitrary", "arbitrary"),
            vmem_limit_bytes=VMEM_LIMIT_BYTES),
        name="gla_layer",
    )(h, p, s0, *weights)


def _pack_rows(rows):
    out = [jnp.pad(r.astype(F32), (0, D_MODEL - r.shape[0])) for r in rows]
    out += [jnp.zeros((D_MODEL,), F32)] * (VEC_ROWS - len(out))
    return jnp.stack(out)


def kernel(x_prompt, x_sample, state_pool, state_gla, p_prompt, p_sample, norm_mix, norm_ffn,
           norm_ple, norm_final, w_pool, b_pool, pool_scale, w_gla_in, w_gla_gate_up, b_gla_gate,
           gla_norm, w_gla_out, w_ffn_gate, w_ffn_up, w_ffn_down, w_ple_proj, w_ple_gate):
    vec0 = _pack_rows([norm_mix[0], norm_ffn[0], norm_ple[0], b_pool[0], pool_scale[0]])[None]
    vec1 = _pack_rows([norm_mix[1], norm_ffn[1], norm_ple[1], b_gla_gate[0], gla_norm[0],
                       norm_final])[None]
    wpool = w_pool.astype(BF16)
    ffn = (w_ffn_gate.astype(BF16), w_ffn_up.astype(BF16), w_ffn_down.astype(BF16),
           w_ple_gate.astype(BF16), w_ple_proj.astype(BF16))
    rank_pad = GATE_RANK_PAD - GLA_GATE_RANK
    win = w_gla_in.astype(BF16)
    wgr = jnp.pad(w_gla_in[:, :, IN_R:].astype(BF16), ((0, 0), (0, 0), (0, rank_pad)))
    wgu = jnp.pad(w_gla_gate_up.astype(BF16), ((0, 0), (0, rank_pad), (0, 0)))
    wo = w_gla_out.astype(BF16)

    bp = x_prompt.shape[0]
    hist0 = jnp.zeros((1, bp, POOL_HIST, D_MODEL), F32)
    s00 = jnp.zeros((1, bp, GLA_HEADS, GLA_DK, GLA_DV), F32)

    def trunk(x, p, hist, s0, start, pool_tile, gla_tile):
        h1, pool_state = _pool_layer(x, p, 0, hist, vec0, wpool, *ffn, start=start, **pool_tile)
        y, gla_state = _gla_layer(h1, p, 1, s0, vec1, win, wgr, wgu, wo, *ffn, **gla_tile)
        return y, pool_state, gla_state

    y_p, ps_p, gs_p = trunk(x_prompt, p_prompt, hist0, s00, 0,
                            dict(nb=1, tm=POOL_PROMPT_TILE, sub=PROMPT_SUB),
                            dict(nb=1, tm=GLA_PROMPT_TILE, sub=PROMPT_SUB))
    bs, ts = x_sample.shape[:2]
    sample_tile = dict(nb=bs, tm=ts, sub=ts)
    y_s, ps_s, gs_s = trunk(x_sample, p_sample, state_pool, state_gla, PAST_LEN,
                            sample_tile, sample_tile)
    return (y_p, y_s, ps_p, ps_s, gs_p, gs_s)
```

```python
import functools

import jax
import jax.numpy as jnp
from jax import lax
from jax.experimental import pallas as pl
from jax.experimental.pallas import tpu as pltpu

D_MODEL = 1024
PAST_LEN = 2048
CHUNK = 64
POOL_WINDOWS = (2, 4, 8, 16)
POOL_GROUP = D_MODEL // len(POOL_WINDOWS)
POOL_HIST = max(POOL_WINDOWS) - 1
HIST_PAD = POOL_HIST + 1
GLA_HEADS = 4
GLA_KEY_DIM = D_MODEL // 2
GLA_VAL_DIM = D_MODEL
GLA_DK = GLA_KEY_DIM // GLA_HEADS
GLA_DV = GLA_VAL_DIM // GLA_HEADS
GLA_GATE_RANK = 16
GLA_GATE_NORMALIZER = 16.0
GATE_RANK_PAD = 128
IN_Q, IN_K, IN_V = 0, GLA_KEY_DIM, 2 * GLA_KEY_DIM
IN_G = IN_V + GLA_VAL_DIM
IN_R = IN_G + GLA_VAL_DIM
D_FF = 2816
PLE_DIM = 256
EPS = 1e-6

VMEM_LIMIT_BYTES = 58 * 1024 * 1024
VMEM_TEMP_BYTES = 8 * 1024 * 1024
FFN_CHUNK = 256
PROMPT_SUB = 256
POOL_PROMPT_TILE = 4 * PROMPT_SUB
GLA_PROMPT_TILE = 2 * PROMPT_SUB

BF16 = jnp.bfloat16
F32 = jnp.float32

V_NORM_MIX, V_NORM_FFN, V_NORM_PLE, V_A, V_B, V_C = 0, 1, 2, 3, 4, 5
VEC_ROWS = 8


def _dot(a, b):
    return jnp.dot(a, b, preferred_element_type=F32)


def _rms(x, g):
    ms = jnp.mean(x * x, axis=-1, keepdims=True)
    return x * lax.rsqrt(ms + EPS) * g


def _run(gen):
    for _ in gen:
        pass


def _interleave(a, b):
    total_a, total_b = next(a), next(b)
    done_a = done_b = 0.0
    live_a = live_b = True
    while live_a or live_b:
        pick_a = live_a and (not live_b or done_a / total_a <= done_b / total_b)
        try:
            if pick_a:
                done_a += next(a)
            else:
                done_b += next(b)
        except StopIteration:
            if pick_a:
                live_a = False
            else:
                live_b = False


def _pipeline(fronts, backs):
    _run(fronts[0])
    for i in range(1, len(fronts)):
        _interleave(backs[i - 1], fronts[i])
    _run(backs[-1])


class _Stream:
    def __init__(self, gen, start, span):
        self.gen = gen
        self.total = next(gen)
        self.start, self.span = start, span
        self.done = 0.0
        self.live = True

    def clock(self):
        return self.start + self.span * self.done / self.total

    def step(self):
        try:
            self.done += next(self.gen)
        except StopIteration:
            self.live = False


def _pipeline_staggered(fronts, backs):
    first = _Stream(fronts[0], 0.0, 0.0)
    first.span = first.total
    streams = [first]
    unit = None
    for i, back in enumerate(backs):
        sb = _Stream(back, 0.0, 0.0)
        unit = sb.total if unit is None else unit
        sb.start, sb.span = first.total + i * unit, 2.0 * unit
        streams.append(sb)
        if i + 1 < len(fronts):
            streams.append(_Stream(fronts[i + 1], first.total + i * unit, unit))
    while True:
        live = [s for s in streams if s.live]
        if not live:
            break
        min(live, key=_Stream.clock).step()


def _ffn_back(residual, p_ref, vec_ref, wg_ref, wu_ref, wd_ref, wpg_ref, wpp_ref, out_ref,
              *, nb, r0, sub, final_norm):
    ms = nb * sub
    yield 11950.0
    h = residual()
    yield 1024.0
    xn = _rms(h, vec_ref[V_NORM_FFN:V_NORM_FFN + 1, :]).astype(BF16)
    yield 400.0
    acc = None
    cols = list(range(0, D_FF, FFN_CHUNK))
    nxt = (_dot(xn, wg_ref[:, 0:FFN_CHUNK]), _dot(xn, wu_ref[:, 0:FFN_CHUNK]))
    for i, c0 in enumerate(cols):
        gate, up = nxt
        if i + 1 < len(cols):
            c1 = cols[i + 1]
            nxt = (_dot(xn, wg_ref[:, c1:c1 + FFN_CHUNK]), _dot(xn, wu_ref[:, c1:c1 + FFN_CHUNK]))
        act = (gate * jax.nn.sigmoid(gate) * up).astype(BF16)
        part = _dot(act, wd_ref[c0:c0 + FFN_CHUNK, :])
        acc = part if acc is None else acc + part
        yield 768.0
    h = h + acc
    xn = _rms(h, vec_ref[V_NORM_PLE:V_NORM_PLE + 1, :]).astype(BF16)
    yield 400.0
    ple_gate = jax.nn.sigmoid(_dot(xn, wpg_ref[...]))
    yield 1024.0
    p_bf = p_ref[:, r0:r0 + sub, :].reshape(ms, PLE_DIM).astype(BF16)
    h = h + ple_gate * _dot(p_bf, wpp_ref[...])
    yield 256.0
    if final_norm:
        h = _rms(h, vec_ref[V_C:V_C + 1, :])
    out_ref[:, r0:r0 + sub, :] = h.reshape(nb, sub, D_MODEL)
    yield 400.0


def _pool_front(x_ref, vec_ref, wpool_ref, buf_ref, d_ref, carry, *, nb, r0, sub, pos0):
    ms = nb * sub
    yield 2400.0
    x = x_ref[:, r0:r0 + sub, :].reshape(ms, D_MODEL)
    xn = _rms(x, vec_ref[V_NORM_MIX:V_NORM_MIX + 1, :])
    buf_ref[:, HIST_PAD + r0:HIST_PAD + r0 + sub, :] = xn.reshape(nb, sub, D_MODEL)
    yield 600.0
    pos = pos0 + r0 + lax.broadcasted_iota(jnp.int32, (sub, 1), 0)
    d0 = r0 * nb
    ys = []
    for gi, w in enumerate(POOL_WINDOWS):
        sl = slice(gi * POOL_GROUP, (gi + 1) * POOL_GROUP)
        inv_cnt = 1.0 / jnp.minimum(w, pos + 1).astype(F32)
        for bi in range(nb):
            ext = buf_ref[bi, r0:r0 + HIST_PAD + sub, sl]
            acc = ext
            span = 1
            while span < w:
                acc = acc + pltpu.roll(acc, span, 0)
                span *= 2
            cur = ext[HIST_PAD:, :]
            d = acc[HIST_PAD:, :] * inv_cnt - cur
            d_ref[d0 + bi * sub:d0 + (bi + 1) * sub, sl] = d.astype(BF16)
        ys.append(_dot(d_ref[d0:d0 + ms, sl], wpool_ref[gi]))
        yield 100.0 * (gi + 2)
    y = jnp.concatenate(ys, axis=1)
    y = (y + vec_ref[V_A:V_A + 1, :]) * vec_ref[V_B:V_B + 1, :]
    carry[r0] = x + y
    yield 400.0


def _pool_layer_kernel(x_ref, p_ref, hist_ref, vec_ref, wpool_ref, wg_ref, wu_ref, wd_ref,
                       wpg_ref, wpp_ref, h_out_ref, state_out_ref, buf_ref, d_ref,
                       *, nb, tm, sub, start):
    t = pl.program_id(1)

    @pl.when(t == 0)
    def _():
        buf_ref[:, 0:1, :] = jnp.zeros((nb, 1, D_MODEL), F32)
        buf_ref[:, 1:HIST_PAD, :] = hist_ref[...]

    carry = {}
    starts = list(range(0, tm, sub))
    fronts = [_pool_front(x_ref, vec_ref, wpool_ref, buf_ref, d_ref, carry,
                          nb=nb, r0=r0, sub=sub, pos0=start + t * tm) for r0 in starts]
    backs = [_ffn_back(functools.partial(carry.get, r0), p_ref, vec_ref, wg_ref, wu_ref, wd_ref,
                       wpg_ref, wpp_ref, h_out_ref, nb=nb, r0=r0, sub=sub, final_norm=False)
             for r0 in starts]
    _pipeline_staggered(fronts, backs)

    @pl.when(t == pl.num_programs(1) - 1)
    def _():
        state_out_ref[...] = buf_ref[:, tm + 1:tm + HIST_PAD, :]

    buf_ref[:, 0:HIST_PAD, :] = buf_ref[:, tm:tm + HIST_PAD, :]


def _vmem_limit(weights, window_bytes, scratch_bytes):
    resident = sum(w.size // w.shape[0] * w.dtype.itemsize for w in weights)
    return min(VMEM_LIMIT_BYTES, resident + 2 * window_bytes + scratch_bytes + VMEM_TEMP_BYTES)


def _layer_spec(shape, layer):
    nd = len(shape) - 1
    return pl.BlockSpec((None,) + tuple(shape[1:]), lambda b, t: (layer,) + (0,) * nd,
                        pipeline_mode=pl.Buffered(1))


def _pool_layer(x, p, layer, hist, vecs, wpool, wg, wu, wd, wpg, wpp, *, nb, tm, sub, start):
    weights = (vecs, wpool, wg, wu, wd, wpg, wpp)
    wlayer = (0, 0, layer, layer, layer, layer, layer)
    B, T, _ = x.shape
    grid = (B // nb, T // tm)
    kern = functools.partial(_pool_layer_kernel, nb=nb, tm=tm, sub=sub, start=start)
    state_spec = pl.BlockSpec((None, nb, POOL_HIST, D_MODEL), lambda b, t: (0, b, 0, 0))
    rows = nb * tm
    vmem_limit = _vmem_limit(
        weights,
        window_bytes=4 * (rows * (2 * D_MODEL + PLE_DIM) + 2 * nb * HIST_PAD * D_MODEL),
        scratch_bytes=4 * nb * (HIST_PAD + tm) * D_MODEL + 2 * rows * D_MODEL)
    return pl.pallas_call(
        kern,
        grid=grid,
        in_specs=[
            pl.BlockSpec((nb, tm, D_MODEL), lambda b, t: (b, t, 0)),
            pl.BlockSpec((None, nb, tm, PLE_DIM), lambda b, t: (layer, b, t, 0)),
            state_spec,
        ] + [_layer_spec(w.shape, l) for w, l in zip(weights, wlayer)],
        out_specs=[
            pl.BlockSpec((nb, tm, D_MODEL), lambda b, t: (b, t, 0)),
            state_spec,
        ],
        out_shape=[
            jax.ShapeDtypeStruct((B, T, D_MODEL), F32),
            jax.ShapeDtypeStruct((1, B, POOL_HIST, D_MODEL), F32),
        ],
        scratch_shapes=[
            pltpu.VMEM((nb, HIST_PAD + tm, D_MODEL), F32),
            pltpu.VMEM((nb * tm, D_MODEL), BF16),
        ],
        compiler_params=pltpu.CompilerParams(
            dimension_semantics=("arbitrary", "arbitrary"),
            vmem_limit_bytes=vmem_limit),
        name="pool_layer",
    )(x, p, hist, *weights)


def _gla_front(h_ref, vec_ref, win_ref, wgr_ref, wgu_ref, s_ref, o_s, state,
               *, nb, r0, sub, chunk):
    ms = nb * sub
    segs_per_batch = sub // chunk
    nseg = nb * segs_per_batch
    half = GLA_VAL_DIM // 2
    yield 9700.0
    h = h_ref[:, r0:r0 + sub, :].reshape(ms, D_MODEL)
    xn = _rms(h, vec_ref[V_NORM_MIX:V_NORM_MIX + 1, :]).astype(BF16)
    yield 400.0
    gr = _dot(xn, wgr_ref[...]).astype(BF16)
    q = _dot(xn, win_ref[:, IN_Q:IN_K]) * (GLA_DK ** -0.5)
    yield 512.0
    z = _dot(gr, wgu_ref[...]) + vec_ref[V_A:V_A + 1, 0:GLA_KEY_DIM]
    log_sig = jnp.minimum(z, 0.0) - jnp.log1p(jnp.exp(-jnp.abs(z)))
    la = log_sig * (1.0 / GLA_GATE_NORMALIZER)
    la_hi = la.astype(BF16)
    la_lo = (la - la_hi.astype(F32)).astype(BF16)
    k = _dot(xn, win_ref[:, IN_K:IN_V])
    yield 512.0
    row = lax.broadcasted_iota(jnp.int32, (ms, ms), 0)
    col = lax.broadcasted_iota(jnp.int32, (ms, ms), 1)
    causal = ((row & (-chunk)) == (col & (-chunk))) & (col <= row)
    tri = causal.astype(BF16)
    b = _dot(tri, la_hi) + _dot(tri, la_lo)
    v_parts = []
    for c0 in (IN_V, IN_V + half):
        v_parts.append(_dot(xn, win_ref[:, c0:c0 + half]).astype(BF16))
        yield 512.0
    v = jnp.concatenate(v_parts, axis=1)
    b_last = [b[(s + 1) * chunk - 1:(s + 1) * chunk, :] for s in range(nseg)]
    b_last_rows = jnp.concatenate(
        [jnp.broadcast_to(bl, (chunk, GLA_KEY_DIM)) for bl in b_last], axis=0)
    q_dec = (q * jnp.exp(b)).astype(BF16)
    k_inv = (k * jnp.exp(-b)).astype(BF16)
    k_end = (k * jnp.exp(b_last_rows - b)).astype(BF16)
    e_last = [jnp.exp(bl) for bl in b_last]
    yield 300.0
    g_parts = []
    for c0 in (IN_G, IN_G + half):
        g = _dot(xn, win_ref[:, c0:c0 + half])
        g_parts.append(g * jax.nn.sigmoid(g))
        yield 512.0
    out_gate = jnp.concatenate(g_parts, axis=1)

    eye = (lax.broadcasted_iota(jnp.int32, (GLA_DK, GLA_DK), 0)
           == lax.broadcasted_iota(jnp.int32, (GLA_DK, GLA_DK), 1)).astype(F32)
    gnorm = vec_ref[V_B:V_B + 1, 0:GLA_DV]
    row_seg = lax.broadcasted_iota(jnp.int32, (ms, GLA_DK), 0) & (-chunk)

    for hh in range(GLA_HEADS):
        ks = slice(hh * GLA_DK, (hh + 1) * GLA_DK)
        vs = slice(hh * GLA_DV, (hh + 1) * GLA_DV)
        qd_h = q_dec[:, ks]
        v_h = v[:, vs]
        scores = lax.dot_general(qd_h, k_inv[:, ks], (((1,), (1,)), ((), ())),
                                 preferred_element_type=F32)
        scores = jnp.where(causal, scores, 0.0).astype(BF16)
        o_intra = _dot(scores, v_h)
        ke_h = k_end[:, ks]
        ke_blk = jnp.concatenate(
            [jnp.where(row_seg == s * chunk, ke_h, jnp.zeros_like(ke_h)) for s in range(nseg)],
            axis=1)
        kv_all = lax.dot_general(ke_blk, v_h, (((0,), (0,)), ((), ())), preferred_element_type=F32)
        yield 700.0
        o_inter = []
        for bi in range(nb):
            if (bi, hh) not in state:
                state[bi, hh] = s_ref[bi, hh]
            for ci in range(segs_per_batch):
                s = bi * segs_per_batch + ci
                rows = slice(s * chunk, (s + 1) * chunk)
                o_inter.append(_dot(qd_h[rows, :], state[bi, hh].astype(BF16)))
                e_col = jnp.sum(eye * e_last[s][:, ks], axis=1, keepdims=True)
                state[bi, hh] = e_col * state[bi, hh] + kv_all[s * GLA_DK:(s + 1) * GLA_DK, :]
        o = o_intra + jnp.concatenate(o_inter, axis=0)
        o = o * lax.rsqrt(jnp.mean(o * o, axis=-1, keepdims=True) + EPS) * gnorm
        o_s[:, r0:r0 + sub, vs] = (o * out_gate[:, vs]).astype(BF16).reshape(nb, sub, GLA_DV)
        yield 700.0


def _gla_residual(h_ref, o_s, wo_ref, *, nb, r0, sub):
    ms = nb * sub
    h = h_ref[:, r0:r0 + sub, :].reshape(ms, D_MODEL)
    return h + _dot(o_s[:, r0:r0 + sub, :].reshape(ms, GLA_VAL_DIM), wo_ref[...])


def _gla_layer_kernel(h_ref, p_ref, s0_ref, vec_ref, win_ref, wgr_ref, wgu_ref, wo_ref,
                      wg_ref, wu_ref, wd_ref, wpg_ref, wpp_ref,
                      y_out_ref, s_ref, o_s, *, nb, tm, sub, chunk):
    t = pl.program_id(1)

    @pl.when(t == 0)
    def _():
        s_ref[...] = s0_ref[...]

    state = {}
    starts = list(range(0, tm, sub))
    fronts = [_gla_front(h_ref, vec_ref, win_ref, wgr_ref, wgu_ref, s_ref, o_s, state,
                         nb=nb, r0=r0, sub=sub, chunk=chunk) for r0 in starts]
    backs = [_ffn_back(functools.partial(_gla_residual, h_ref, o_s, wo_ref, nb=nb, r0=r0, sub=sub),
                       p_ref, vec_ref, wg_ref, wu_ref, wd_ref, wpg_ref, wpp_ref, y_out_ref,
                       nb=nb, r0=r0, sub=sub, final_norm=True) for r0 in starts]
    _pipeline(fronts, backs)
    for (bi, hh), s in state.items():
        s_ref[bi, hh] = s


def _gla_layer(h, p, layer, s0, vecs, win, wgr, wgu, wo, wg, wu, wd, wpg, wpp, *, nb, tm, sub):
    B, T, _ = h.shape
    chunk = min(CHUNK, T)
    grid = (B // nb, T // tm)
    kern = functools.partial(_gla_layer_kernel, nb=nb, tm=tm, sub=sub, chunk=chunk)
    weights = (vecs, win, wgr, wgu, wo, wg, wu, wd, wpg, wpp)
    wlayer = (0, 0, 0, 0, 0, layer, layer, layer, layer, layer)
    state_block = (None, nb, GLA_HEADS, GLA_DK, GLA_DV)

    def state_index(b, t):
        return (0, b, 0, 0, 0)

    rows = nb * tm
    state_bytes = 4 * nb * GLA_HEADS * GLA_DK * GLA_DV
    vmem_limit = _vmem_limit(
        weights,
        window_bytes=4 * rows * (2 * D_MODEL + PLE_DIM) + 2 * state_bytes,
        scratch_bytes=2 * rows * GLA_VAL_DIM)

    return pl.pallas_call(
        kern,
        grid=grid,
        in_specs=[
            pl.BlockSpec((nb, tm, D_MODEL), lambda b, t: (b, t, 0)),
            pl.BlockSpec((None, nb, tm, PLE_DIM), lambda b, t: (layer, b, t, 0)),
            pl.BlockSpec(state_block, state_index, pipeline_mode=pl.Buffered(1)),
        ] + [_layer_spec(w.shape, l) for w, l in zip(weights, wlayer)],
        out_specs=[
            pl.BlockSpec((nb, tm, D_MODEL), lambda b, t: (b, t, 0)),
            pl.BlockSpec(state_block, state_index),
        ],
        out_shape=[
            jax.ShapeDtypeStruct((B, T, D_MODEL), F32),
            jax.ShapeDtypeStruct((1, B, GLA_HEADS, GLA_DK, GLA_DV), F32),
        ],
        scratch_shapes=[
            pltpu.VMEM((nb, tm, GLA_VAL_DIM), BF16),
        ],
        compiler_params=pltpu.CompilerParams(
            dimension_semantics=("arbitrary", "arbitrary"),
            vmem_limit_bytes=vmem_limit),
        name="gla_layer",
    )(h, p, s0, *weights)


def _pack_rows(rows):
    out = [jnp.pad(r.astype(F32), (0, D_MODEL - r.shape[0])) for r in rows]
    out += [jnp.zeros((D_MODEL,), F32)] * (VEC_ROWS - len(out))
    return jnp.stack(out)


def kernel(x_prompt, x_sample, state_pool, state_gla, p_prompt, p_sample, norm_mix, norm_ffn,
           norm_ple, norm_final, w_pool, b_pool, pool_scale, w_gla_in, w_gla_gate_up, b_gla_gate,
           gla_norm, w_gla_out, w_ffn_gate, w_ffn_up, w_ffn_down, w_ple_proj, w_ple_gate):
    vec0 = _pack_rows([norm_mix[0], norm_ffn[0], norm_ple[0], b_pool[0], pool_scale[0]])[None]
    vec1 = _pack_rows([norm_mix[1], norm_ffn[1], norm_ple[1], b_gla_gate[0], gla_norm[0],
                       norm_final])[None]
    wpool = w_pool.astype(BF16)
    ffn = (w_ffn_gate.astype(BF16), w_ffn_up.astype(BF16), w_ffn_down.astype(BF16),
           w_ple_gate.astype(BF16), w_ple_proj.astype(BF16))
    rank_pad = GATE_RANK_PAD - GLA_GATE_RANK
    win = w_gla_in.astype(BF16)
    wgr = jnp.pad(w_gla_in[:, :, IN_R:].astype(BF16), ((0, 0), (0, 0), (0, rank_pad)))
    wgu = jnp.pad(w_gla_gate_up.astype(BF16), ((0, 0), (0, rank_pad), (0, 0)))
    wo = w_gla_out.astype(BF16)

    bp = x_prompt.shape[0]
    hist0 = jnp.zeros((1, bp, POOL_HIST, D_MODEL), F32)
    s00 = jnp.zeros((1, bp, GLA_HEADS, GLA_DK, GLA_DV), F32)

    def trunk(x, p, hist, s0, start, pool_tile, gla_tile):
        h1, pool_state = _pool_layer(x, p, 0, hist, vec0, wpool, *ffn, start=start, **pool_tile)
        y, gla_state = _gla_layer(h1, p, 1, s0, vec1, win, wgr, wgu, wo, *ffn, **gla_tile)
        return y, pool_state, gla_state

    y_p, ps_p, gs_p = trunk(x_prompt, p_prompt, hist0, s00, 0,
                            dict(nb=1, tm=POOL_PROMPT_TILE, sub=PROMPT_SUB),
                            dict(nb=1, tm=GLA_PROMPT_TILE, sub=PROMPT_SUB))
    bs, ts = x_sample.shape[:2]
    sample_tile = dict(nb=bs, tm=ts, sub=ts)
    y_s, ps_s, gs_s = trunk(x_sample, p_sample, state_pool, state_gla, PAST_LEN,
                            sample_tile, sample_tile)
    return (y_p, y_s, ps_p, ps_s, gs_p, gs_s)
```

```python
import functools

import jax
import jax.numpy as jnp
from jax import lax
from jax.experimental import pallas as pl
from jax.experimental.pallas import tpu as pltpu

D_MODEL = 1024
PAST_LEN = 2048
CHUNK = 64
POOL_WINDOWS = (2, 4, 8, 16)
POOL_GROUP = D_MODEL // len(POOL_WINDOWS)
POOL_HIST = max(POOL_WINDOWS) - 1
HIST_PAD = POOL_HIST + 1
GLA_HEADS = 4
GLA_KEY_DIM = D_MODEL // 2
GLA_VAL_DIM = D_MODEL
GLA_DK = GLA_KEY_DIM // GLA_HEADS
GLA_DV = GLA_VAL_DIM // GLA_HEADS
GLA_GATE_RANK = 16
GLA_GATE_NORMALIZER = 16.0
GATE_RANK_PAD = 128
IN_Q, IN_K, IN_V = 0, GLA_KEY_DIM, 2 * GLA_KEY_DIM
IN_G = IN_V + GLA_VAL_DIM
IN_R = IN_G + GLA_VAL_DIM
D_FF = 2816
PLE_DIM = 256
EPS = 1e-6

VMEM_LIMIT_BYTES = 58 * 1024 * 1024
VMEM_TEMP_BYTES = 6 * 1024 * 1024
FFN_CHUNK = 256
PROMPT_SUB = 256
POOL_PROMPT_TILE = 4 * PROMPT_SUB
GLA_PROMPT_TILE = 2 * PROMPT_SUB

BF16 = jnp.bfloat16
F32 = jnp.float32

V_NORM_MIX, V_NORM_FFN, V_NORM_PLE, V_A, V_B, V_C = 0, 1, 2, 3, 4, 5
VEC_ROWS = 8


def _dot(a, b):
    return jnp.dot(a, b, preferred_element_type=F32)


def _rms(x, g):
    ms = jnp.mean(x * x, axis=-1, keepdims=True)
    return x * lax.rsqrt(ms + EPS) * g


def _run(gen):
    for _ in gen:
        pass


def _interleave(a, b):
    total_a, total_b = next(a), next(b)
    done_a = done_b = 0.0
    live_a = live_b = True
    while live_a or live_b:
        pick_a = live_a and (not live_b or done_a / total_a <= done_b / total_b)
        try:
            if pick_a:
                done_a += next(a)
            else:
                done_b += next(b)
        except StopIteration:
            if pick_a:
                live_a = False
            else:
                live_b = False


def _pipeline(fronts, backs):
    _run(fronts[0])
    for i in range(1, len(fronts)):
        _interleave(backs[i - 1], fronts[i])
    _run(backs[-1])


class _Stream:
    def __init__(self, gen, start, span):
        self.gen = gen
        self.total = next(gen)
        self.start, self.span = start, span
        self.done = 0.0
        self.live = True

    def clock(self):
        return self.start + self.span * self.done / self.total

    def step(self):
        try:
            self.done += next(self.gen)
        except StopIteration:
            self.live = False


def _pipeline_staggered(fronts, backs):
    first = _Stream(fronts[0], 0.0, 0.0)
    first.span = first.total
    streams = [first]
    unit = None
    for i, back in enumerate(backs):
        sb = _Stream(back, 0.0, 0.0)
        unit = sb.total if unit is None else unit
        sb.start, sb.span = first.total + i * unit, 2.0 * unit
        streams.append(sb)
        if i + 1 < len(fronts):
            streams.append(_Stream(fronts[i + 1], first.total + i * unit, unit))
    while True:
        live = [s for s in streams if s.live]
        if not live:
            break
        min(live, key=_Stream.clock).step()


def _ffn_back(residual, p_ref, vec_ref, wg_ref, wu_ref, wd_ref, wpg_ref, wpp_ref, out_ref,
              *, nb, r0, sub, final_norm):
    ms = nb * sub
    yield 11950.0
    h = residual()
    yield 1024.0
    xn = _rms(h, vec_ref[V_NORM_FFN:V_NORM_FFN + 1, :]).astype(BF16)
    yield 400.0
    acc = None
    cols = list(range(0, D_FF, FFN_CHUNK))
    nxt = (_dot(xn, wg_ref[:, 0:FFN_CHUNK]), _dot(xn, wu_ref[:, 0:FFN_CHUNK]))
    for i, c0 in enumerate(cols):
        gate, up = nxt
        if i + 1 < len(cols):
            c1 = cols[i + 1]
            nxt = (_dot(xn, wg_ref[:, c1:c1 + FFN_CHUNK]), _dot(xn, wu_ref[:, c1:c1 + FFN_CHUNK]))
        act = (gate * jax.nn.sigmoid(gate) * up).astype(BF16)
        part = _dot(act, wd_ref[c0:c0 + FFN_CHUNK, :])
        acc = part if acc is None else acc + part
        yield 768.0
    h = h + acc
    xn = _rms(h, vec_ref[V_NORM_PLE:V_NORM_PLE + 1, :]).astype(BF16)
    yield 400.0
    ple_gate = jax.nn.sigmoid(_dot(xn, wpg_ref[...]))
    yield 1024.0
    p_bf = p_ref[:, r0:r0 + sub, :].reshape(ms, PLE_DIM).astype(BF16)
    h = h + ple_gate * _dot(p_bf, wpp_ref[...])
    yield 256.0
    if final_norm:
        h = _rms(h, vec_ref[V_C:V_C + 1, :])
    out_ref[:, r0:r0 + sub, :] = h.reshape(nb, sub, D_MODEL)
    yield 400.0


def _pool_front(x_ref, vec_ref, wpool_ref, buf_ref, d_ref, carry, *, nb, r0, sub, pos0):
    ms = nb * sub
    yield 2400.0
    x = x_ref[:, r0:r0 + sub, :].reshape(ms, D_MODEL)
    xn = _rms(x, vec_ref[V_NORM_MIX:V_NORM_MIX + 1, :])
    buf_ref[:, HIST_PAD + r0:HIST_PAD + r0 + sub, :] = xn.reshape(nb, sub, D_MODEL)
    yield 600.0
    pos = pos0 + r0 + lax.broadcasted_iota(jnp.int32, (sub, 1), 0)
    d0 = r0 * nb
    ys = []
    for gi, w in enumerate(POOL_WINDOWS):
        sl = slice(gi * POOL_GROUP, (gi + 1) * POOL_GROUP)
        inv_cnt = 1.0 / jnp.minimum(w, pos + 1).astype(F32)
        for bi in range(nb):
            ext = buf_ref[bi, r0:r0 + HIST_PAD + sub, sl]
            acc = ext
            span = 1
            while span < w:
                acc = acc + pltpu.roll(acc, span, 0)
                span *= 2
            cur = ext[HIST_PAD:, :]
            d = acc[HIST_PAD:, :] * inv_cnt - cur
            d_ref[d0 + bi * sub:d0 + (bi + 1) * sub, sl] = d.astype(BF16)
        ys.append(_dot(d_ref[d0:d0 + ms, sl], wpool_ref[gi]))
        yield 100.0 * (gi + 2)
    y = jnp.concatenate(ys, axis=1)
    y = (y + vec_ref[V_A:V_A + 1, :]) * vec_ref[V_B:V_B + 1, :]
    carry[r0] = x + y
    yield 400.0


def _pool_layer_kernel(x_ref, p_ref, hist_ref, vec_ref, wpool_ref, wg_ref, wu_ref, wd_ref,
                       wpg_ref, wpp_ref, h_out_ref, state_out_ref, buf_ref, d_ref,
                       *, nb, tm, sub, start):
    t = pl.program_id(1)

    @pl.when(t == 0)
    def _():
        buf_ref[:, 0:1, :] = jnp.zeros((nb, 1, D_MODEL), F32)
        buf_ref[:, 1:HIST_PAD, :] = hist_ref[...]

    carry = {}
    starts = list(range(0, tm, sub))
    fronts = [_pool_front(x_ref, vec_ref, wpool_ref, buf_ref, d_ref, carry,
                          nb=nb, r0=r0, sub=sub, pos0=start + t * tm) for r0 in starts]
    backs = [_ffn_back(functools.partial(carry.get, r0), p_ref, vec_ref, wg_ref, wu_ref, wd_ref,
                       wpg_ref, wpp_ref, h_out_ref, nb=nb, r0=r0, sub=sub, final_norm=False)
             for r0 in starts]
    _pipeline_staggered(fronts, backs)

    @pl.when(t == pl.num_programs(1) - 1)
    def _():
        state_out_ref[...] = buf_ref[:, tm + 1:tm + HIST_PAD, :]

    buf_ref[:, 0:HIST_PAD, :] = buf_ref[:, tm:tm + HIST_PAD, :]


def _vmem_limit(weights, window_bytes, scratch_bytes):
    resident = sum(w.size // w.shape[0] * w.dtype.itemsize for w in weights)
    return min(VMEM_LIMIT_BYTES, resident + 2 * window_bytes + scratch_bytes + VMEM_TEMP_BYTES)


def _layer_spec(shape, layer):
    nd = len(shape) - 1
    return pl.BlockSpec((None,) + tuple(shape[1:]), lambda b, t: (layer,) + (0,) * nd,
                        pipeline_mode=pl.Buffered(1))


def _pool_layer(x, p, layer, hist, vecs, wpool, wg, wu, wd, wpg, wpp, *, nb, tm, sub, start):
    weights = (vecs, wpool, wg, wu, wd, wpg, wpp)
    wlayer = (0, 0, layer, layer, layer, layer, layer)
    B, T, _ = x.shape
    grid = (B // nb, T // tm)
    kern = functools.partial(_pool_layer_kernel, nb=nb, tm=tm, sub=sub, start=start)
    state_spec = pl.BlockSpec((None, nb, POOL_HIST, D_MODEL), lambda b, t: (0, b, 0, 0))
    rows = nb * tm
    vmem_limit = _vmem_limit(
        weights,
        window_bytes=4 * (rows * (2 * D_MODEL + PLE_DIM) + 2 * nb * HIST_PAD * D_MODEL),
        scratch_bytes=4 * nb * (HIST_PAD + tm) * D_MODEL + 2 * rows * D_MODEL)
    return pl.pallas_call(
        kern,
        grid=grid,
        in_specs=[
            pl.BlockSpec((nb, tm, D_MODEL), lambda b, t: (b, t, 0)),
            pl.BlockSpec((None, nb, tm, PLE_DIM), lambda b, t: (layer, b, t, 0)),
            state_spec,
        ] + [_layer_spec(w.shape, l) for w, l in zip(weights, wlayer)],
        out_specs=[
            pl.BlockSpec((nb, tm, D_MODEL), lambda b, t: (b, t, 0)),
            state_spec,
        ],
        out_shape=[
            jax.ShapeDtypeStruct((B, T, D_MODEL), F32),
            jax.ShapeDtypeStruct((1, B, POOL_HIST, D_MODEL), F32),
        ],
        scratch_shapes=[
            pltpu.VMEM((nb, HIST_PAD + tm, D_MODEL), F32),
            pltpu.VMEM((nb * tm, D_MODEL), BF16),
        ],
        compiler_params=pltpu.CompilerParams(
            dimension_semantics=("arbitrary", "arbitrary"),
            vmem_limit_bytes=vmem_limit),
        name="pool_layer",
    )(x, p, hist, *weights)


def _gla_front(h_ref, vec_ref, win_ref, wgr_ref, wgu_ref, s_ref, o_s, state,
               *, nb, r0, sub, chunk):
    ms = nb * sub
    segs_per_batch = sub // chunk
    nseg = nb * segs_per_batch
    half = GLA_VAL_DIM // 2
    yield 9700.0
    h = h_ref[:, r0:r0 + sub, :].reshape(ms, D_MODEL)
    xn = _rms(h, vec_ref[V_NORM_MIX:V_NORM_MIX + 1, :]).astype(BF16)
    yield 400.0
    gr = _dot(xn, wgr_ref[...]).astype(BF16)
    q = _dot(xn, win_ref[:, IN_Q:IN_K]) * (GLA_DK ** -0.5)
    yield 512.0
    z = _dot(gr, wgu_ref[...]) + vec_ref[V_A:V_A + 1, 0:GLA_KEY_DIM]
    log_sig = jnp.minimum(z, 0.0) - jnp.log1p(jnp.exp(-jnp.abs(z)))
    la = log_sig * (1.0 / GLA_GATE_NORMALIZER)
    la_hi = la.astype(BF16)
    la_lo = (la - la_hi.astype(F32)).astype(BF16)
    k = _dot(xn, win_ref[:, IN_K:IN_V])
    yield 512.0
    row = lax.broadcasted_iota(jnp.int32, (ms, ms), 0)
    col = lax.broadcasted_iota(jnp.int32, (ms, ms), 1)
    causal = ((row & (-chunk)) == (col & (-chunk))) & (col <= row)
    tri = causal.astype(BF16)
    b = _dot(tri, la_hi) + _dot(tri, la_lo)
    v_parts = []
    for c0 in (IN_V, IN_V + half):
        v_parts.append(_dot(xn, win_ref[:, c0:c0 + half]).astype(BF16))
        yield 512.0
    v = jnp.concatenate(v_parts, axis=1)
    b_last = [b[(s + 1) * chunk - 1:(s + 1) * chunk, :] for s in range(nseg)]
    b_last_rows = jnp.concatenate(
        [jnp.broadcast_to(bl, (chunk, GLA_KEY_DIM)) for bl in b_last], axis=0)
    q_dec = (q * jnp.exp(b)).astype(BF16)
    k_inv = (k * jnp.exp(-b)).astype(BF16)
    k_end = (k * jnp.exp(b_last_rows - b)).astype(BF16)
    e_last = [jnp.exp(bl) for bl in b_last]
    yield 300.0
    g_parts = []
    for c0 in (IN_G, IN_G + half):
        g = _dot(xn, win_ref[:, c0:c0 + half])
        g_parts.append(g * jax.nn.sigmoid(g))
        yield 512.0
    out_gate = jnp.concatenate(g_parts, axis=1)

    eye = (lax.broadcasted_iota(jnp.int32, (GLA_DK, GLA_DK), 0)
           == lax.broadcasted_iota(jnp.int32, (GLA_DK, GLA_DK), 1)).astype(F32)
    gnorm = vec_ref[V_B:V_B + 1, 0:GLA_DV]
    row_seg = lax.broadcasted_iota(jnp.int32, (ms, GLA_DK), 0) & (-chunk)

    for hh in range(GLA_HEADS):
        ks = slice(hh * GLA_DK, (hh + 1) * GLA_DK)
        vs = slice(hh * GLA_DV, (hh + 1) * GLA_DV)
        qd_h = q_dec[:, ks]
        v_h = v[:, vs]
        scores = lax.dot_general(qd_h, k_inv[:, ks], (((1,), (1,)), ((), ())),
                                 preferred_element_type=F32)
        scores = jnp.where(causal, scores, 0.0).astype(BF16)
        o_intra = _dot(scores, v_h)
        ke_h = k_end[:, ks]
        ke_blk = jnp.concatenate(
            [jnp.where(row_seg == s * chunk, ke_h, jnp.zeros_like(ke_h)) for s in range(nseg)],
            axis=1)
        kv_all = lax.dot_general(ke_blk, v_h, (((0,), (0,)), ((), ())), preferred_element_type=F32)
        yield 700.0
        o_inter = []
        for bi in range(nb):
            if (bi, hh) not in state:
                state[bi, hh] = s_ref[bi, hh]
            for ci in range(segs_per_batch):
                s = bi * segs_per_batch + ci
                rows = slice(s * chunk, (s + 1) * chunk)
                o_inter.append(_dot(qd_h[rows, :], state[bi, hh].astype(BF16)))
                e_col = jnp.sum(eye * e_last[s][:, ks], axis=1, keepdims=True)
                state[bi, hh] = e_col * state[bi, hh] + kv_all[s * GLA_DK:(s + 1) * GLA_DK, :]
        o = o_intra + jnp.concatenate(o_inter, axis=0)
        o = o * lax.rsqrt(jnp.mean(o * o, axis=-1, keepdims=True) + EPS) * gnorm
        o_s[:, r0:r0 + sub, vs] = (o * out_gate[:, vs]).astype(BF16).reshape(nb, sub, GLA_DV)
        yield 700.0


def _gla_residual(h_ref, o_s, wo_ref, *, nb, r0, sub):
    ms = nb * sub
    h = h_ref[:, r0:r0 + sub, :].reshape(ms, D_MODEL)
    return h + _dot(o_s[:, r0:r0 + sub, :].reshape(ms, GLA_VAL_DIM), wo_ref[...])


def _gla_layer_kernel(h_ref, p_ref, s0_ref, vec_ref, win_ref, wgr_ref, wgu_ref, wo_ref,
                      wg_ref, wu_ref, wd_ref, wpg_ref, wpp_ref,
                      y_out_ref, s_ref, o_s, *, nb, tm, sub, chunk):
    t = pl.program_id(1)

    @pl.when(t == 0)
    def _():
        s_ref[...] = s0_ref[...]

    state = {}
    starts = list(range(0, tm, sub))
    fronts = [_gla_front(h_ref, vec_ref, win_ref, wgr_ref, wgu_ref, s_ref, o_s, state,
                         nb=nb, r0=r0, sub=sub, chunk=chunk) for r0 in starts]
    backs = [_ffn_back(functools.partial(_gla_residual, h_ref, o_s, wo_ref, nb=nb, r0=r0, sub=sub),
                       p_ref, vec_ref, wg_ref, wu_ref, wd_ref, wpg_ref, wpp_ref, y_out_ref,
                       nb=nb, r0=r0, sub=sub, final_norm=True) for r0 in starts]
    _pipeline(fronts, backs)
    for (bi, hh), s in state.items():
        s_ref[bi, hh] = s


def _gla_layer(h, p, layer, s0, vecs, win, wgr, wgu, wo, wg, wu, wd, wpg, wpp, *, nb, tm, sub):
    B, T, _ = h.shape
    chunk = min(CHUNK, T)
    grid = (B // nb, T // tm)
    kern = functools.partial(_gla_layer_kernel, nb=nb, tm=tm, sub=sub, chunk=chunk)
    weights = (vecs, win, wgr, wgu, wo, wg, wu, wd, wpg, wpp)
    wlayer = (0, 0, 0, 0, 0, layer, layer, layer, layer, layer)
    state_block = (None, nb, GLA_HEADS, GLA_DK, GLA_DV)

    def state_index(b, t):
        return (0, b, 0, 0, 0)

    rows = nb * tm
    state_bytes = 4 * nb * GLA_HEADS * GLA_DK * GLA_DV
    vmem_limit = _vmem_limit(
        weights,
        window_bytes=4 * rows * (2 * D_MODEL + PLE_DIM) + 2 * state_bytes,
        scratch_bytes=2 * rows * GLA_VAL_DIM)

    return pl.pallas_call(
        kern,
        grid=grid,
        in_specs=[
            pl.BlockSpec((nb, tm, D_MODEL), lambda b, t: (b, t, 0)),
            pl.BlockSpec((None, nb, tm, PLE_DIM), lambda b, t: (layer, b, t, 0)),
            pl.BlockSpec(state_block, state_index, pipeline_mode=pl.Buffered(1)),
        ] + [_layer_spec(w.shape, l) for w, l in zip(weights, wlayer)],
        out_specs=[
            pl.BlockSpec((nb, tm, D_MODEL), lambda b, t: (b, t, 0)),
            pl.BlockSpec(state_block, state_index),
        ],
        out_shape=[
            jax.ShapeDtypeStruct((B, T, D_MODEL), F32),
            jax.ShapeDtypeStruct((1, B, GLA_HEADS, GLA_DK, GLA_DV), F32),
        ],
        scratch_shapes=[
            pltpu.VMEM((nb, tm, GLA_VAL_DIM), BF16),
        ],
        compiler_params=pltpu.CompilerParams(
            dimension_semantics=("arbitrary", "arbitrary"),
            vmem_limit_bytes=vmem_limit),
        name="gla_layer",
    )(h, p, s0, *weights)


def _pack_rows(rows):
    out = [jnp.pad(r.astype(F32), (0, D_MODEL - r.shape[0])) for r in rows]
    out += [jnp.zeros((D_MODEL,), F32)] * (VEC_ROWS - len(out))
    return jnp.stack(out)


def kernel(x_prompt, x_sample, state_pool, state_gla, p_prompt, p_sample, norm_mix, norm_ffn,
           norm_ple, norm_final, w_pool, b_pool, pool_scale, w_gla_in, w_gla_gate_up, b_gla_gate,
           gla_norm, w_gla_out, w_ffn_gate, w_ffn_up, w_ffn_down, w_ple_proj, w_ple_gate):
    vec0 = _pack_rows([norm_mix[0], norm_ffn[0], norm_ple[0], b_pool[0], pool_scale[0]])[None]
    vec1 = _pack_rows([norm_mix[1], norm_ffn[1], norm_ple[1], b_gla_gate[0], gla_norm[0],
                       norm_final])[None]
    wpool = w_pool.astype(BF16)
    ffn = (w_ffn_gate.astype(BF16), w_ffn_up.astype(BF16), w_ffn_down.astype(BF16),
           w_ple_gate.astype(BF16), w_ple_proj.astype(BF16))
    rank_pad = GATE_RANK_PAD - GLA_GATE_RANK
    win = w_gla_in.astype(BF16)
    wgr = jnp.pad(w_gla_in[:, :, IN_R:].astype(BF16), ((0, 0), (0, 0), (0, rank_pad)))
    wgu = jnp.pad(w_gla_gate_up.astype(BF16), ((0, 0), (0, rank_pad), (0, 0)))
    wo = w_gla_out.astype(BF16)

    bp = x_prompt.shape[0]
    hist0 = jnp.zeros((1, bp, POOL_HIST, D_MODEL), F32)
    s00 = jnp.zeros((1, bp, GLA_HEADS, GLA_DK, GLA_DV), F32)

    def trunk(x, p, hist, s0, start, pool_tile, gla_tile):
        h1, pool_state = _pool_layer(x, p, 0, hist, vec0, wpool, *ffn, start=start, **pool_tile)
        y, gla_state = _gla_layer(h1, p, 1, s0, vec1, win, wgr, wgu, wo, *ffn, **gla_tile)
        return y, pool_state, gla_state

    y_p, ps_p, gs_p = trunk(x_prompt, p_prompt, hist0, s00, 0,
                            dict(nb=1, tm=POOL_PROMPT_TILE, sub=PROMPT_SUB),
                            dict(nb=1, tm=GLA_PROMPT_TILE, sub=PROMPT_SUB))
    bs, ts = x_sample.shape[:2]
    sample_tile = dict(nb=bs, tm=ts, sub=ts)
    y_s, ps_s, gs_s = trunk(x_sample, p_sample, state_pool, state_gla, PAST_LEN,
                            sample_tile, sample_tile)
    return (y_p, y_s, ps_p, ps_s, gs_p, gs_s)
```

```python
import functools

import jax
import jax.numpy as jnp
from jax import lax
from jax.experimental import pallas as pl
from jax.experimental.pallas import tpu as pltpu

D_MODEL = 1024
PAST_LEN = 2048
CHUNK = 64
POOL_WINDOWS = (2, 4, 8, 16)
POOL_GROUP = D_MODEL // len(POOL_WINDOWS)
POOL_HIST = max(POOL_WINDOWS) - 1
HIST_PAD = POOL_HIST + 1
GLA_HEADS = 4
GLA_KEY_DIM = D_MODEL // 2
GLA_VAL_DIM = D_MODEL
GLA_DK = GLA_KEY_DIM // GLA_HEADS
GLA_DV = GLA_VAL_DIM // GLA_HEADS
GLA_GATE_RANK = 16
GLA_GATE_NORMALIZER = 16.0
GATE_RANK_PAD = 128
IN_Q, IN_K, IN_V = 0, GLA_KEY_DIM, 2 * GLA_KEY_DIM
IN_G = IN_V + GLA_VAL_DIM
IN_R = IN_G + GLA_VAL_DIM
D_FF = 2816
PLE_DIM = 256
EPS = 1e-6

VMEM_LIMIT_BYTES = 58 * 1024 * 1024
VMEM_TEMP_BYTES = 8 * 1024 * 1024
FFN_CHUNK = 256
PROMPT_SUB = 256
POOL_PROMPT_TILE = 4 * PROMPT_SUB
GLA_PROMPT_TILE = 2 * PROMPT_SUB

BF16 = jnp.bfloat16
F32 = jnp.float32

V_NORM_MIX, V_NORM_FFN, V_NORM_PLE, V_A, V_B, V_C = 0, 1, 2, 3, 4, 5
VEC_ROWS = 8


def _dot(a, b):
    return jnp.dot(a, b, preferred_element_type=F32)


def _rms(x, g):
    ms = jnp.mean(x * x, axis=-1, keepdims=True)
    return x * lax.rsqrt(ms + EPS) * g


def _run(gen):
    for _ in gen:
        pass


def _interleave(a, b):
    total_a, total_b = next(a), next(b)
    done_a = done_b = 0.0
    live_a = live_b = True
    while live_a or live_b:
        pick_a = live_a and (not live_b or done_a / total_a <= done_b / total_b)
        try:
            if pick_a:
                done_a += next(a)
            else:
                done_b += next(b)
        except StopIteration:
            if pick_a:
                live_a = False
            else:
                live_b = False


def _pipeline(fronts, backs):
    _run(fronts[0])
    for i in range(1, len(fronts)):
        _interleave(backs[i - 1], fronts[i])
    _run(backs[-1])


class _Stream:
    def __init__(self, gen, start, span):
        self.gen = gen
        self.total = next(gen)
        self.start, self.span = start, span
        self.done = 0.0
        self.live = True

    def clock(self):
        return self.start + self.span * self.done / self.total

    def step(self):
        try:
            self.done += next(self.gen)
        except StopIteration:
            self.live = False


def _pipeline_staggered(fronts, backs):
    first = _Stream(fronts[0], 0.0, 0.0)
    first.span = first.total
    streams = [first]
    unit = None
    for i, back in enumerate(backs):
        sb = _Stream(back, 0.0, 0.0)
        unit = sb.total if unit is None else unit
        sb.start, sb.span = first.total + i * unit, 2.0 * unit
        streams.append(sb)
        if i + 1 < len(fronts):
            streams.append(_Stream(fronts[i + 1], first.total + i * unit, unit))
    while True:
        live = [s for s in streams if s.live]
        if not live:
            break
        min(live, key=_Stream.clock).step()


def _ffn_back(residual, p_ref, vec_ref, wg_ref, wu_ref, wd_ref, wpg_ref, wpp_ref, out_ref,
              *, nb, r0, sub, final_norm):
    ms = nb * sub
    yield 11950.0
    h = residual()
    yield 1024.0
    xn = _rms(h, vec_ref[V_NORM_FFN:V_NORM_FFN + 1, :]).astype(BF16)
    yield 400.0
    acc = None
    cols = list(range(0, D_FF, FFN_CHUNK))
    nxt = (_dot(xn, wg_ref[:, 0:FFN_CHUNK]), _dot(xn, wu_ref[:, 0:FFN_CHUNK]))
    for i, c0 in enumerate(cols):
        gate, up = nxt
        if i + 1 < len(cols):
            c1 = cols[i + 1]
            nxt = (_dot(xn, wg_ref[:, c1:c1 + FFN_CHUNK]), _dot(xn, wu_ref[:, c1:c1 + FFN_CHUNK]))
        act = (gate * jax.nn.sigmoid(gate) * up).astype(BF16)
        part = _dot(act, wd_ref[c0:c0 + FFN_CHUNK, :])
        acc = part if acc is None else acc + part
        yield 768.0
    h = h + acc
    xn = _rms(h, vec_ref[V_NORM_PLE:V_NORM_PLE + 1, :]).astype(BF16)
    yield 400.0
    ple_gate = jax.nn.sigmoid(_dot(xn, wpg_ref[...]))
    yield 1024.0
    p_bf = p_ref[:, r0:r0 + sub, :].reshape(ms, PLE_DIM).astype(BF16)
    h = h + ple_gate * _dot(p_bf, wpp_ref[...])
    yield 256.0
    if final_norm:
        h = _rms(h, vec_ref[V_C:V_C + 1, :])
    out_ref[:, r0:r0 + sub, :] = h.reshape(nb, sub, D_MODEL)
    yield 400.0


def _pool_front(x_ref, vec_ref, wpool_ref, buf_ref, d_ref, carry, *, nb, r0, sub, pos0):
    ms = nb * sub
    yield 2400.0
    x = x_ref[:, r0:r0 + sub, :].reshape(ms, D_MODEL)
    xn = _rms(x, vec_ref[V_NORM_MIX:V_NORM_MIX + 1, :])
    buf_ref[:, HIST_PAD + r0:HIST_PAD + r0 + sub, :] = xn.reshape(nb, sub, D_MODEL)
    yield 600.0
    pos = pos0 + r0 + lax.broadcasted_iota(jnp.int32, (sub, 1), 0)
    d0 = r0 * nb
    ys = []
    for gi, w in enumerate(POOL_WINDOWS):
        sl = slice(gi * POOL_GROUP, (gi + 1) * POOL_GROUP)
        inv_cnt = 1.0 / jnp.minimum(w, pos + 1).astype(F32)
        for bi in range(nb):
            ext = buf_ref[bi, r0:r0 + HIST_PAD + sub, sl]
            acc = ext
            span = 1
            while span < w:
                acc = acc + pltpu.roll(acc, span, 0)
                span *= 2
            cur = ext[HIST_PAD:, :]
            d = acc[HIST_PAD:, :] * inv_cnt - cur
            d_ref[d0 + bi * sub:d0 + (bi + 1) * sub, sl] = d.astype(BF16)
        ys.append(_dot(d_ref[d0:d0 + ms, sl], wpool_ref[gi]))
        yield 100.0 * (gi + 2)
    y = jnp.concatenate(ys, axis=1)
    y = (y + vec_ref[V_A:V_A + 1, :]) * vec_ref[V_B:V_B + 1, :]
    carry[r0] = x + y
    yield 400.0


def _pool_layer_kernel(x_ref, p_ref, hist_ref, vec_ref, wpool_ref, wg_ref, wu_ref, wd_ref,
                       wpg_ref, wpp_ref, h_out_ref, state_out_ref, buf_ref, d_ref,
                       *, nb, tm, sub, start):
    t = pl.program_id(1)

    @pl.when(t == 0)
    def _():
        buf_ref[:, 0:1, :] = jnp.zeros((nb, 1, D_MODEL), F32)
        buf_ref[:, 1:HIST_PAD, :] = hist_ref[...]

    carry = {}
    starts = list(range(0, tm, sub))
    fronts = [_pool_front(x_ref, vec_ref, wpool_ref, buf_ref, d_ref, carry,
                          nb=nb, r0=r0, sub=sub, pos0=start + t * tm) for r0 in starts]
    backs = [_ffn_back(functools.partial(carry.get, r0), p_ref, vec_ref, wg_ref, wu_ref, wd_ref,
                       wpg_ref, wpp_ref, h_out_ref, nb=nb, r0=r0, sub=sub, final_norm=False)
             for r0 in starts]
    _pipeline_staggered(fronts, backs)

    @pl.when(t == pl.num_programs(1) - 1)
    def _():
        state_out_ref[...] = buf_ref[:, tm + 1:tm + HIST_PAD, :]

    buf_ref[:, 0:HIST_PAD, :] = buf_ref[:, tm:tm + HIST_PAD, :]


def _vmem_limit(weights, window_bytes, scratch_bytes):
    resident = sum(w.size // w.shape[0] * w.dtype.itemsize for w in weights)
    return min(VMEM_LIMIT_BYTES, resident + 2 * window_bytes + scratch_bytes + VMEM_TEMP_BYTES)


def _layer_spec(shape, layer):
    nd = len(shape) - 1
    return pl.BlockSpec((None,) + tuple(shape[1:]), lambda b, t: (layer,) + (0,) * nd,
                        pipeline_mode=pl.Buffered(1))


def _pool_layer(x, p, layer, hist, vecs, wpool, wg, wu, wd, wpg, wpp, *, nb, tm, sub, start):
    weights = (vecs, wpool, wg, wu, wd, wpg, wpp)
    wlayer = (0, 0, layer, layer, layer, layer, layer)
    B, T, _ = x.shape
    grid = (B // nb, T // tm)
    kern = functools.partial(_pool_layer_kernel, nb=nb, tm=tm, sub=sub, start=start)
    state_spec = pl.BlockSpec((None, nb, POOL_HIST, D_MODEL), lambda b, t: (0, b, 0, 0))
    rows = nb * tm
    vmem_limit = _vmem_limit(
        weights,
        window_bytes=4 * (rows * (2 * D_MODEL + PLE_DIM) + 2 * nb * HIST_PAD * D_MODEL),
        scratch_bytes=4 * nb * (HIST_PAD + tm) * D_MODEL + 2 * rows * D_MODEL)
    return pl.pallas_call(
        kern,
        grid=grid,
        in_specs=[
            pl.BlockSpec((nb, tm, D_MODEL), lambda b, t: (b, t, 0)),
            pl.BlockSpec((None, nb, tm, PLE_DIM), lambda b, t: (layer, b, t, 0)),
            state_spec,
        ] + [_layer_spec(w.shape, l) for w, l in zip(weights, wlayer)],
        out_specs=[
            pl.BlockSpec((nb, tm, D_MODEL), lambda b, t: (b, t, 0)),
            state_spec,
        ],
        out_shape=[
            jax.ShapeDtypeStruct((B, T, D_MODEL), F32),
            jax.ShapeDtypeStruct((1, B, POOL_HIST, D_MODEL), F32),
        ],
        scratch_shapes=[
            pltpu.VMEM((nb, HIST_PAD + tm, D_MODEL), F32),
            pltpu.VMEM((nb * tm, D_MODEL), BF16),
        ],
        compiler_params=pltpu.CompilerParams(
            dimension_semantics=("arbitrary", "arbitrary"),
            vmem_limit_bytes=vmem_limit),
        name="pool_layer",
    )(x, p, hist, *weights)


def _gla_front(h_ref, vec_ref, win_ref, wgr_ref, wgu_ref, s_ref, o_s, state,
               *, nb, r0, sub, chunk):
    ms = nb * sub
    segs_per_batch = sub // chunk
    nseg = nb * segs_per_batch
    half = GLA_VAL_DIM // 2
    yield 9700.0
    h = h_ref[:, r0:r0 + sub, :].reshape(ms, D_MODEL)
    xn = _rms(h, vec_ref[V_NORM_MIX:V_NORM_MIX + 1, :]).astype(BF16)
    yield 400.0
    gr = _dot(xn, wgr_ref[...]).astype(BF16)
    qk = _dot(xn, win_ref[:, IN_Q:IN_V])
    q = qk[:, 0:GLA_KEY_DIM] * (GLA_DK ** -0.5)
    k = qk[:, GLA_KEY_DIM:]
    yield 1024.0
    z = _dot(gr, wgu_ref[...]) + vec_ref[V_A:V_A + 1, 0:GLA_KEY_DIM]
    log_sig = jnp.minimum(z, 0.0) - jnp.log1p(jnp.exp(-jnp.abs(z)))
    la = log_sig * (1.0 / GLA_GATE_NORMALIZER)
    la_hi = la.astype(BF16)
    la_lo = (la - la_hi.astype(F32)).astype(BF16)
    row = lax.broadcasted_iota(jnp.int32, (ms, ms), 0)
    col = lax.broadcasted_iota(jnp.int32, (ms, ms), 1)
    causal = ((row & (-chunk)) == (col & (-chunk))) & (col <= row)
    tri = causal.astype(BF16)
    b = _dot(tri, la_hi) + _dot(tri, la_lo)
    v = _dot(xn, win_ref[:, IN_V:IN_G]).astype(BF16)
    yield 1024.0
    b_last = [b[(s + 1) * chunk - 1:(s + 1) * chunk, :] for s in range(nseg)]
    b_last_rows = jnp.concatenate(
        [jnp.broadcast_to(bl, (chunk, GLA_KEY_DIM)) for bl in b_last], axis=0)
    q_dec = (q * jnp.exp(b)).astype(BF16)
    k_inv = (k * jnp.exp(-b)).astype(BF16)
    k_end = (k * jnp.exp(b_last_rows - b)).astype(BF16)
    e_last = [jnp.exp(bl) for bl in b_last]
    yield 300.0
    g = _dot(xn, win_ref[:, IN_G:IN_R])
    out_gate = g * jax.nn.sigmoid(g)
    yield 1024.0

    eye = (lax.broadcasted_iota(jnp.int32, (GLA_DK, GLA_DK), 0)
           == lax.broadcasted_iota(jnp.int32, (GLA_DK, GLA_DK), 1)).astype(F32)
    gnorm = vec_ref[V_B:V_B + 1, 0:GLA_DV]
    row_seg = lax.broadcasted_iota(jnp.int32, (ms, GLA_DK), 0) & (-chunk)

    for hh in range(GLA_HEADS):
        ks = slice(hh * GLA_DK, (hh + 1) * GLA_DK)
        vs = slice(hh * GLA_DV, (hh + 1) * GLA_DV)
        qd_h = q_dec[:, ks]
        v_h = v[:, vs]
        scores = lax.dot_general(qd_h, k_inv[:, ks], (((1,), (1,)), ((), ())),
                                 preferred_element_type=F32)
        scores = jnp.where(causal, scores, 0.0).astype(BF16)
        o_intra = _dot(scores, v_h)
        ke_h = k_end[:, ks]
        ke_blk = jnp.concatenate(
            [jnp.where(row_seg == s * chunk, ke_h, jnp.zeros_like(ke_h)) for s in range(nseg)],
            axis=1)
        kv_all = lax.dot_general(ke_blk, v_h, (((0,), (0,)), ((), ())), preferred_element_type=F32)
        yield 700.0
        o_inter = []
        for bi in range(nb):
            if (bi, hh) not in state:
                state[bi, hh] = s_ref[bi, hh]
            for ci in range(segs_per_batch):
                s = bi * segs_per_batch + ci
                rows = slice(s * chunk, (s + 1) * chunk)
                o_inter.append(_dot(qd_h[rows, :], state[bi, hh].astype(BF16)))
                e_col = jnp.sum(eye * e_last[s][:, ks], axis=1, keepdims=True)
                state[bi, hh] = e_col * state[bi, hh] + kv_all[s * GLA_DK:(s + 1) * GLA_DK, :]
        o = o_intra + jnp.concatenate(o_inter, axis=0)
        o = o * lax.rsqrt(jnp.mean(o * o, axis=-1, keepdims=True) + EPS) * gnorm
        o_s[:, r0:r0 + sub, vs] = (o * out_gate[:, vs]).astype(BF16).reshape(nb, sub, GLA_DV)
        yield 700.0


def _gla_residual(h_ref, o_s, wo_ref, *, nb, r0, sub):
    ms = nb * sub
    h = h_ref[:, r0:r0 + sub, :].reshape(ms, D_MODEL)
    return h + _dot(o_s[:, r0:r0 + sub, :].reshape(ms, GLA_VAL_DIM), wo_ref[...])


def _gla_layer_kernel(h_ref, p_ref, s0_ref, vec_ref, win_ref, wgr_ref, wgu_ref, wo_ref,
                      wg_ref, wu_ref, wd_ref, wpg_ref, wpp_ref,
                      y_out_ref, s_ref, o_s, *, nb, tm, sub, chunk):
    t = pl.program_id(1)

    @pl.when(t == 0)
    def _():
        s_ref[...] = s0_ref[...]

    state = {}
    starts = list(range(0, tm, sub))
    fronts = [_gla_front(h_ref, vec_ref, win_ref, wgr_ref, wgu_ref, s_ref, o_s, state,
                         nb=nb, r0=r0, sub=sub, chunk=chunk) for r0 in starts]
    backs = [_ffn_back(functools.partial(_gla_residual, h_ref, o_s, wo_ref, nb=nb, r0=r0, sub=sub),
                       p_ref, vec_ref, wg_ref, wu_ref, wd_ref, wpg_ref, wpp_ref, y_out_ref,
                       nb=nb, r0=r0, sub=sub, final_norm=True) for r0 in starts]
    _pipeline(fronts, backs)
    for (bi, hh), s in state.items():
        s_ref[bi, hh] = s


def _gla_layer(h, p, layer, s0, vecs, win, wgr, wgu, wo, wg, wu, wd, wpg, wpp, *, nb, tm, sub):
    B, T, _ = h.shape
    chunk = min(CHUNK, T)
    grid = (B // nb, T // tm)
    kern = functools.partial(_gla_layer_kernel, nb=nb, tm=tm, sub=sub, chunk=chunk)
    weights = (vecs, win, wgr, wgu, wo, wg, wu, wd, wpg, wpp)
    wlayer = (0, 0, 0, 0, 0, layer, layer, layer, layer, layer)
    state_block = (None, nb, GLA_HEADS, GLA_DK, GLA_DV)

    def state_index(b, t):
        return (0, b, 0, 0, 0)

    rows = nb * tm
    state_bytes = 4 * nb * GLA_HEADS * GLA_DK * GLA_DV
    vmem_limit = _vmem_limit(
        weights,
        window_bytes=4 * rows * (2 * D_MODEL + PLE_DIM) + 2 * state_bytes,
        scratch_bytes=2 * rows * GLA_VAL_DIM)

    return pl.pallas_call(
        kern,
        grid=grid,
        in_specs=[
            pl.BlockSpec((nb, tm, D_MODEL), lambda b, t: (b, t, 0)),
            pl.BlockSpec((None, nb, tm, PLE_DIM), lambda b, t: (layer, b, t, 0)),
            pl.BlockSpec(state_block, state_index, pipeline_mode=pl.Buffered(1)),
        ] + [_layer_spec(w.shape, l) for w, l in zip(weights, wlayer)],
        out_specs=[
            pl.BlockSpec((nb, tm, D_MODEL), lambda b, t: (b, t, 0)),
            pl.BlockSpec(state_block, state_index),
        ],
        out_shape=[
            jax.ShapeDtypeStruct((B, T, D_MODEL), F32),
            jax.ShapeDtypeStruct((1, B, GLA_HEADS, GLA_DK, GLA_DV), F32),
        ],
        scratch_shapes=[
            pltpu.VMEM((nb, tm, GLA_VAL_DIM), BF16),
        ],
        compiler_params=pltpu.CompilerParams(
            dimension_semantics=("arbitrary", "arbitrary"),
            vmem_limit_bytes=vmem_limit),
        name="gla_layer",
    )(h, p, s0, *weights)


def _pack_rows(rows):
    out = [jnp.pad(r.astype(F32), (0, D_MODEL - r.shape[0])) for r in rows]
    out += [jnp.zeros((D_MODEL,), F32)] * (VEC_ROWS - len(out))
    return jnp.stack(out)


def kernel(x_prompt, x_sample, state_pool, state_gla, p_prompt, p_sample, norm_mix, norm_ffn,
           norm_ple, norm_final, w_pool, b_pool, pool_scale, w_gla_in, w_gla_gate_up, b_gla_gate,
           gla_norm, w_gla_out, w_ffn_gate, w_ffn_up, w_ffn_down, w_ple_proj, w_ple_gate):
    vec0 = _pack_rows([norm_mix[0], norm_ffn[0], norm_ple[0], b_pool[0], pool_scale[0]])[None]
    vec1 = _pack_rows([norm_mix[1], norm_ffn[1], norm_ple[1], b_gla_gate[0], gla_norm[0],
                       norm_final])[None]
    wpool = w_pool.astype(BF16)
    ffn = (w_ffn_gate.astype(BF16), w_ffn_up.astype(BF16), w_ffn_down.astype(BF16),
           w_ple_gate.astype(BF16), w_ple_proj.astype(BF16))
    rank_pad = GATE_RANK_PAD - GLA_GATE_RANK
    win = w_gla_in.astype(BF16)
    wgr = jnp.pad(w_gla_in[:, :, IN_R:].astype(BF16), ((0, 0), (0, 0), (0, rank_pad)))
    wgu = jnp.pad(w_gla_gate_up.astype(BF16), ((0, 0), (0, rank_pad), (0, 0)))
    wo = w_gla_out.astype(BF16)

    bp = x_prompt.shape[0]
    hist0 = jnp.zeros((1, bp, POOL_HIST, D_MODEL), F32)
    s00 = jnp.zeros((1, bp, GLA_HEADS, GLA_DK, GLA_DV), F32)

    def trunk(x, p, hist, s0, start, pool_tile, gla_tile):
        h1, pool_state = _pool_layer(x, p, 0, hist, vec0, wpool, *ffn, start=start, **pool_tile)
        y, gla_state = _gla_layer(h1, p, 1, s0, vec1, win, wgr, wgu, wo, *ffn, **gla_tile)
        return y, pool_state, gla_state

    y_p, ps_p, gs_p = trunk(x_prompt, p_prompt, hist0, s00, 0,
                            dict(nb=1, tm=POOL_PROMPT_TILE, sub=PROMPT_SUB),
                            dict(nb=1, tm=GLA_PROMPT_TILE, sub=PROMPT_SUB))
    bs, ts = x_sample.shape[:2]
    sample_tile = dict(nb=bs, tm=ts, sub=ts)
    y_s, ps_s, gs_s = trunk(x_sample, p_sample, state_pool, state_gla, PAST_LEN,
                            sample_tile, sample_tile)
    return (y_p, y_s, ps_p, ps_s, gs_p, gs_s)
```

```python
import functools

import jax
import jax.numpy as jnp
from jax import lax
from jax.experimental import pallas as pl
from jax.experimental.pallas import tpu as pltpu

D_MODEL = 1024
PAST_LEN = 2048
CHUNK = 64
POOL_WINDOWS = (2, 4, 8, 16)
POOL_GROUP = D_MODEL // len(POOL_WINDOWS)
POOL_HIST = max(POOL_WINDOWS) - 1
HIST_PAD = POOL_HIST + 1
GLA_HEADS = 4
GLA_KEY_DIM = D_MODEL // 2
GLA_VAL_DIM = D_MODEL
GLA_DK = GLA_KEY_DIM // GLA_HEADS
GLA_DV = GLA_VAL_DIM // GLA_HEADS
GLA_GATE_RANK = 16
GLA_GATE_NORMALIZER = 16.0
GATE_RANK_PAD = 128
IN_Q, IN_K, IN_V = 0, GLA_KEY_DIM, 2 * GLA_KEY_DIM
IN_G = IN_V + GLA_VAL_DIM
IN_R = IN_G + GLA_VAL_DIM
D_FF = 2816
PLE_DIM = 256
EPS = 1e-6

VMEM_LIMIT_BYTES = 58 * 1024 * 1024
VMEM_TEMP_BYTES = 8 * 1024 * 1024
FFN_CHUNK = 256
PROMPT_SUB = 256
POOL_PROMPT_TILE = 4 * PROMPT_SUB
GLA_PROMPT_TILE = 2 * PROMPT_SUB

BF16 = jnp.bfloat16
F32 = jnp.float32

V_NORM_MIX, V_NORM_FFN, V_NORM_PLE, V_A, V_B, V_C = 0, 1, 2, 3, 4, 5
VEC_ROWS = 8


def _dot(a, b):
    return jnp.dot(a, b, preferred_element_type=F32)


def _rms(x, g):
    ms = jnp.mean(x * x, axis=-1, keepdims=True)
    return x * lax.rsqrt(ms + EPS) * g


def _run(gen):
    for _ in gen:
        pass


def _interleave(a, b):
    total_a, total_b = next(a), next(b)
    done_a = done_b = 0.0
    live_a = live_b = True
    while live_a or live_b:
        pick_a = live_a and (not live_b or done_a / total_a <= done_b / total_b)
        try:
            if pick_a:
                done_a += next(a)
            else:
                done_b += next(b)
        except StopIteration:
            if pick_a:
                live_a = False
            else:
                live_b = False


def _pipeline(fronts, backs):
    _run(fronts[0])
    for i in range(1, len(fronts)):
        _interleave(backs[i - 1], fronts[i])
    _run(backs[-1])


class _Stream:
    def __init__(self, gen, start, span):
        self.gen = gen
        self.total = next(gen)
        self.start, self.span = start, span
        self.done = 0.0
        self.live = True

    def clock(self):
        return self.start + self.span * self.done / self.total

    def step(self):
        try:
            self.done += next(self.gen)
        except StopIteration:
            self.live = False


def _pipeline_staggered(fronts, backs):
    first = _Stream(fronts[0], 0.0, 0.0)
    first.span = first.total
    streams = [first]
    unit = None
    for i, back in enumerate(backs):
        sb = _Stream(back, 0.0, 0.0)
        unit = sb.total if unit is None else unit
        sb.start, sb.span = first.total + i * unit, 2.0 * unit
        streams.append(sb)
        if i + 1 < len(fronts):
            streams.append(_Stream(fronts[i + 1], first.total + i * unit, unit))
    while True:
        live = [s for s in streams if s.live]
        if not live:
            break
        min(live, key=_Stream.clock).step()


def _ffn_back(residual, p_ref, vec_ref, wg_ref, wu_ref, wd_ref, wpg_ref, wpp_ref, out_ref,
              *, nb, r0, sub, final_norm):
    ms = nb * sub
    yield 11950.0
    h = residual()
    yield 1024.0
    xn = _rms(h, vec_ref[V_NORM_FFN:V_NORM_FFN + 1, :]).astype(BF16)
    yield 400.0
    acc = None
    cols = list(range(0, D_FF, FFN_CHUNK))
    nxt = (_dot(xn, wg_ref[:, 0:FFN_CHUNK]), _dot(xn, wu_ref[:, 0:FFN_CHUNK]))
    for i, c0 in enumerate(cols):
        gate, up = nxt
        if i + 1 < len(cols):
            c1 = cols[i + 1]
            nxt = (_dot(xn, wg_ref[:, c1:c1 + FFN_CHUNK]), _dot(xn, wu_ref[:, c1:c1 + FFN_CHUNK]))
        act = (gate * jax.nn.sigmoid(gate) * up).astype(BF16)
        part = _dot(act, wd_ref[c0:c0 + FFN_CHUNK, :])
        acc = part if acc is None else acc + part
        yield 768.0
    h = h + acc
    xn = _rms(h, vec_ref[V_NORM_PLE:V_NORM_PLE + 1, :]).astype(BF16)
    yield 400.0
    ple_gate = jax.nn.sigmoid(_dot(xn, wpg_ref[...]))
    yield 1024.0
    p_bf = p_ref[:, r0:r0 + sub, :].reshape(ms, PLE_DIM).astype(BF16)
    h = h + ple_gate * _dot(p_bf, wpp_ref[...])
    yield 256.0
    if final_norm:
        h = _rms(h, vec_ref[V_C:V_C + 1, :])
    out_ref[:, r0:r0 + sub, :] = h.reshape(nb, sub, D_MODEL)
    yield 400.0


def _pool_front(x_ref, vec_ref, wpool_ref, buf_ref, d_ref, carry, *, nb, r0, sub, pos0):
    ms = nb * sub
    yield 2400.0
    x = x_ref[:, r0:r0 + sub, :].reshape(ms, D_MODEL)
    xn = _rms(x, vec_ref[V_NORM_MIX:V_NORM_MIX + 1, :])
    buf_ref[:, HIST_PAD + r0:HIST_PAD + r0 + sub, :] = xn.reshape(nb, sub, D_MODEL)
    yield 600.0
    pos = pos0 + r0 + lax.broadcasted_iota(jnp.int32, (sub, 1), 0)
    d0 = r0 * nb
    ys = []
    for gi, w in enumerate(POOL_WINDOWS):
        sl = slice(gi * POOL_GROUP, (gi + 1) * POOL_GROUP)
        inv_cnt = 1.0 / jnp.minimum(w, pos + 1).astype(F32)
        for bi in range(nb):
            ext = buf_ref[bi, r0:r0 + HIST_PAD + sub, sl]
            acc = ext
            span = 1
            while span < w:
                acc = acc + pltpu.roll(acc, span, 0)
                span *= 2
            cur = ext[HIST_PAD:, :]
            d = acc[HIST_PAD:, :] * inv_cnt - cur
            d_ref[d0 + bi * sub:d0 + (bi + 1) * sub, sl] = d.astype(BF16)
        ys.append(_dot(d_ref[d0:d0 + ms, sl], wpool_ref[gi]))
        yield 100.0 * (gi + 2)
    y = jnp.concatenate(ys, axis=1)
    y = (y + vec_ref[V_A:V_A + 1, :]) * vec_ref[V_B:V_B + 1, :]
    carry[r0] = x + y
    yield 400.0


def _pool_layer_kernel(x_ref, p_ref, hist_ref, vec_ref, wpool_ref, wg_ref, wu_ref, wd_ref,
                       wpg_ref, wpp_ref, h_out_ref, state_out_ref, buf_ref, d_ref,
                       *, nb, tm, sub, start):
    t = pl.program_id(1)

    @pl.when(t == 0)
    def _():
        buf_ref[:, 0:1, :] = jnp.zeros((nb, 1, D_MODEL), F32)
        buf_ref[:, 1:HIST_PAD, :] = hist_ref[...]

    carry = {}
    starts = list(range(0, tm, sub))
    fronts = [_pool_front(x_ref, vec_ref, wpool_ref, buf_ref, d_ref, carry,
                          nb=nb, r0=r0, sub=sub, pos0=start + t * tm) for r0 in starts]
    backs = [_ffn_back(functools.partial(carry.get, r0), p_ref, vec_ref, wg_ref, wu_ref, wd_ref,
                       wpg_ref, wpp_ref, h_out_ref, nb=nb, r0=r0, sub=sub, final_norm=False)
             for r0 in starts]
    _pipeline_staggered(fronts, backs)

    @pl.when(t == pl.num_programs(1) - 1)
    def _():
        state_out_ref[...] = buf_ref[:, tm + 1:tm + HIST_PAD, :]

    buf_ref[:, 0:HIST_PAD, :] = buf_ref[:, tm:tm + HIST_PAD, :]


def _vmem_limit(weights, window_bytes, scratch_bytes):
    resident = sum(w.size // w.shape[0] * w.dtype.itemsize for w in weights)
    return min(VMEM_LIMIT_BYTES, resident + 2 * window_bytes + scratch_bytes + VMEM_TEMP_BYTES)


def _layer_spec(shape, layer):
    nd = len(shape) - 1
    return pl.BlockSpec((None,) + tuple(shape[1:]), lambda b, t: (layer,) + (0,) * nd,
                        pipeline_mode=pl.Buffered(1))


def _pool_layer(x, p, layer, hist, vecs, wpool, wg, wu, wd, wpg, wpp, *, nb, tm, sub, start):
    weights = (vecs, wpool, wg, wu, wd, wpg, wpp)
    wlayer = (0, 0, layer, layer, layer, layer, layer)
    B, T, _ = x.shape
    grid = (B // nb, T // tm)
    kern = functools.partial(_pool_layer_kernel, nb=nb, tm=tm, sub=sub, start=start)
    state_spec = pl.BlockSpec((None, nb, POOL_HIST, D_MODEL), lambda b, t: (0, b, 0, 0))
    rows = nb * tm
    vmem_limit = _vmem_limit(
        weights,
        window_bytes=4 * (rows * (2 * D_MODEL + PLE_DIM) + 2 * nb * HIST_PAD * D_MODEL),
        scratch_bytes=4 * nb * (HIST_PAD + tm) * D_MODEL + 2 * rows * D_MODEL)
    return pl.pallas_call(
        kern,
        grid=grid,
        in_specs=[
            pl.BlockSpec((nb, tm, D_MODEL), lambda b, t: (b, t, 0)),
            pl.BlockSpec((None, nb, tm, PLE_DIM), lambda b, t: (layer, b, t, 0)),
            state_spec,
        ] + [_layer_spec(w.shape, l) for w, l in zip(weights, wlayer)],
        out_specs=[
            pl.BlockSpec((nb, tm, D_MODEL), lambda b, t: (b, t, 0)),
            state_spec,
        ],
        out_shape=[
            jax.ShapeDtypeStruct((B, T, D_MODEL), F32),
            jax.ShapeDtypeStruct((1, B, POOL_HIST, D_MODEL), F32),
        ],
        scratch_shapes=[
            pltpu.VMEM((nb, HIST_PAD + tm, D_MODEL), F32),
            pltpu.VMEM((nb * tm, D_MODEL), BF16),
        ],
        compiler_params=pltpu.CompilerParams(
            dimension_semantics=("arbitrary", "arbitrary"),
            vmem_limit_bytes=vmem_limit),
        name="pool_layer",
    )(x, p, hist, *weights)


def _gla_front(h_ref, vec_ref, win_ref, wgr_ref, wgu_ref, s_ref, o_s, state,
               *, nb, r0, sub, chunk):
    ms = nb * sub
    segs_per_batch = sub // chunk
    nseg = nb * segs_per_batch
    half = GLA_VAL_DIM // 2
    yield 9700.0
    h = h_ref[:, r0:r0 + sub, :].reshape(ms, D_MODEL)
    xn = _rms(h, vec_ref[V_NORM_MIX:V_NORM_MIX + 1, :]).astype(BF16)
    yield 400.0
    gr = _dot(xn, wgr_ref[...]).astype(BF16)
    q = _dot(xn, win_ref[:, IN_Q:IN_K]) * (GLA_DK ** -0.5)
    yield 512.0
    z = _dot(gr, wgu_ref[...]) + vec_ref[V_A:V_A + 1, 0:GLA_KEY_DIM]
    log_sig = jnp.minimum(z, 0.0) - jnp.log1p(jnp.exp(-jnp.abs(z)))
    la = log_sig * (1.0 / GLA_GATE_NORMALIZER)
    la_hi = la.astype(BF16)
    la_lo = (la - la_hi.astype(F32)).astype(BF16)
    k = _dot(xn, win_ref[:, IN_K:IN_V])
    yield 512.0
    row = lax.broadcasted_iota(jnp.int32, (ms, ms), 0)
    col = lax.broadcasted_iota(jnp.int32, (ms, ms), 1)
    causal = ((row & (-chunk)) == (col & (-chunk))) & (col <= row)
    tri = causal.astype(BF16)
    b = _dot(tri, la_hi) + _dot(tri, la_lo)
    v_parts = []
    for c0 in (IN_V, IN_V + half):
        v_parts.append(_dot(xn, win_ref[:, c0:c0 + half]).astype(BF16))
        yield 512.0
    v = jnp.concatenate(v_parts, axis=1)
    b_last = [b[(s + 1) * chunk - 1:(s + 1) * chunk, :] for s in range(nseg)]
    b_last_rows = jnp.concatenate(
        [jnp.broadcast_to(bl, (chunk, GLA_KEY_DIM)) for bl in b_last], axis=0)
    q_dec = (q * jnp.exp(b)).astype(BF16)
    k_inv = (k * jnp.exp(-b)).astype(BF16)
    k_end = (k * jnp.exp(b_last_rows - b)).astype(BF16)
    e_last = [jnp.exp(bl) for bl in b_last]
    yield 300.0
    g_parts = []
    for c0 in (IN_G, IN_G + half):
        g = _dot(xn, win_ref[:, c0:c0 + half])
        g_parts.append(g * jax.nn.sigmoid(g))
        yield 512.0
    out_gate = jnp.concatenate(g_parts, axis=1)

    eye = (lax.broadcasted_iota(jnp.int32, (GLA_DK, GLA_DK), 0)
           == lax.broadcasted_iota(jnp.int32, (GLA_DK, GLA_DK), 1)).astype(F32)
    gnorm = vec_ref[V_B:V_B + 1, 0:GLA_DV]
    row_seg = lax.broadcasted_iota(jnp.int32, (ms, GLA_DK), 0) & (-chunk)

    for hh in range(GLA_HEADS):
        ks = slice(hh * GLA_DK, (hh + 1) * GLA_DK)
        vs = slice(hh * GLA_DV, (hh + 1) * GLA_DV)
        qd_h = q_dec[:, ks]
        v_h = v[:, vs]
        scores = lax.dot_general(qd_h, k_inv[:, ks], (((1,), (1,)), ((), ())),
                                 preferred_element_type=F32)
        scores = jnp.where(causal, scores, 0.0).astype(BF16)
        o_intra = _dot(scores, v_h)
        ke_h = k_end[:, ks]
        ke_blk = jnp.concatenate(
            [jnp.where(row_seg == s * chunk, ke_h, jnp.zeros_like(ke_h)) for s in range(nseg)],
            axis=1)
        kv_all = lax.dot_general(ke_blk, v_h, (((0,), (0,)), ((), ())), preferred_element_type=F32)
        yield 700.0
        seg_states = []
        for bi in range(nb):
            if (bi, hh) not in state:
                state[bi, hh] = s_ref[bi, hh]
            for ci in range(segs_per_batch):
                s = bi * segs_per_batch + ci
                seg_states.append(state[bi, hh].astype(BF16))
                e_col = jnp.sum(eye * e_last[s][:, ks], axis=1, keepdims=True)
                state[bi, hh] = e_col * state[bi, hh] + kv_all[s * GLA_DK:(s + 1) * GLA_DK, :]
        qd_blk = jnp.concatenate(
            [jnp.where(row_seg == s * chunk, qd_h, jnp.zeros_like(qd_h)) for s in range(nseg)],
            axis=1)
        o = o_intra + _dot(qd_blk, jnp.concatenate(seg_states, axis=0))
        o = o * lax.rsqrt(jnp.mean(o * o, axis=-1, keepdims=True) + EPS) * gnorm
        o_s[:, r0:r0 + sub, vs] = (o * out_gate[:, vs]).astype(BF16).reshape(nb, sub, GLA_DV)
        yield 700.0


def _gla_residual(h_ref, o_s, wo_ref, *, nb, r0, sub):
    ms = nb * sub
    h = h_ref[:, r0:r0 + sub, :].reshape(ms, D_MODEL)
    return h + _dot(o_s[:, r0:r0 + sub, :].reshape(ms, GLA_VAL_DIM), wo_ref[...])


def _gla_layer_kernel(h_ref, p_ref, s0_ref, vec_ref, win_ref, wgr_ref, wgu_ref, wo_ref,
                      wg_ref, wu_ref, wd_ref, wpg_ref, wpp_ref,
                      y_out_ref, s_ref, o_s, *, nb, tm, sub, chunk):
    t = pl.program_id(1)

    @pl.when(t == 0)
    def _():
        s_ref[...] = s0_ref[...]

    state = {}
    starts = list(range(0, tm, sub))
    fronts = [_gla_front(h_ref, vec_ref, win_ref, wgr_ref, wgu_ref, s_ref, o_s, state,
                         nb=nb, r0=r0, sub=sub, chunk=chunk) for r0 in starts]
    backs = [_ffn_back(functools.partial(_gla_residual, h_ref, o_s, wo_ref, nb=nb, r0=r0, sub=sub),
                       p_ref, vec_ref, wg_ref, wu_ref, wd_ref, wpg_ref, wpp_ref, y_out_ref,
                       nb=nb, r0=r0, sub=sub, final_norm=True) for r0 in starts]
    _pipeline(fronts, backs)
    for (bi, hh), s in state.items():
        s_ref[bi, hh] = s


def _gla_layer(h, p, layer, s0, vecs, win, wgr, wgu, wo, wg, wu, wd, wpg, wpp, *, nb, tm, sub):
    B, T, _ = h.shape
    chunk = min(CHUNK, T)
    grid = (B // nb, T // tm)
    kern = functools.partial(_gla_layer_kernel, nb=nb, tm=tm, sub=sub, chunk=chunk)
    weights = (vecs, win, wgr, wgu, wo, wg, wu, wd, wpg, wpp)
    wlayer = (0, 0, 0, 0, 0, layer, layer, layer, layer, layer)
    state_block = (None, nb, GLA_HEADS, GLA_DK, GLA_DV)

    def state_index(b, t):
        return (0, b, 0, 0, 0)

    rows = nb * tm
    state_bytes = 4 * nb * GLA_HEADS * GLA_DK * GLA_DV
    vmem_limit = _vmem_limit(
        weights,
        window_bytes=4 * rows * (2 * D_MODEL + PLE_DIM) + 2 * state_bytes,
        scratch_bytes=2 * rows * GLA_VAL_DIM)

    return pl.pallas_call(
        kern,
        grid=grid,
        in_specs=[
            pl.BlockSpec((nb, tm, D_MODEL), lambda b, t: (b, t, 0)),
            pl.BlockSpec((None, nb, tm, PLE_DIM), lambda b, t: (layer, b, t, 0)),
            pl.BlockSpec(state_block, state_index, pipeline_mode=pl.Buffered(1)),
        ] + [_layer_spec(w.shape, l) for w, l in zip(weights, wlayer)],
        out_specs=[
            pl.BlockSpec((nb, tm, D_MODEL), lambda b, t: (b, t, 0)),
            pl.BlockSpec(state_block, state_index),
        ],
        out_shape=[
            jax.ShapeDtypeStruct((B, T, D_MODEL), F32),
            jax.ShapeDtypeStruct((1, B, GLA_HEADS, GLA_DK, GLA_DV), F32),
        ],
        scratch_shapes=[
            pltpu.VMEM((nb, tm, GLA_VAL_DIM), BF16),
        ],
        compiler_params=pltpu.CompilerParams(
            dimension_semantics=("arbitrary", "arbitrary"),
            vmem_limit_bytes=vmem_limit),
        name="gla_layer",
    )(h, p, s0, *weights)


def _pack_rows(rows):
    out = [jnp.pad(r.astype(F32), (0, D_MODEL - r.shape[0])) for r in rows]
    out += [jnp.zeros((D_MODEL,), F32)] * (VEC_ROWS - len(out))
    return jnp.stack(out)


def kernel(x_prompt, x_sample, state_pool, state_gla, p_prompt, p_sample, norm_mix, norm_ffn,
           norm_ple, norm_final, w_pool, b_pool, pool_scale, w_gla_in, w_gla_gate_up, b_gla_gate,
           gla_norm, w_gla_out, w_ffn_gate, w_ffn_up, w_ffn_down, w_ple_proj, w_ple_gate):
    vec0 = _pack_rows([norm_mix[0], norm_ffn[0], norm_ple[0], b_pool[0], pool_scale[0]])[None]
    vec1 = _pack_rows([norm_mix[1], norm_ffn[1], norm_ple[1], b_gla_gate[0], gla_norm[0],
                       norm_final])[None]
    wpool = w_pool.astype(BF16)
    ffn = (w_ffn_gate.astype(BF16), w_ffn_up.astype(BF16), w_ffn_down.astype(BF16),
           w_ple_gate.astype(BF16), w_ple_proj.astype(BF16))
    rank_pad = GATE_RANK_PAD - GLA_GATE_RANK
    win = w_gla_in.astype(BF16)
    wgr = jnp.pad(w_gla_in[:, :, IN_R:].astype(BF16), ((0, 0), (0, 0), (0, rank_pad)))
    wgu = jnp.pad(w_gla_gate_up.astype(BF16), ((0, 0), (0, rank_pad), (0, 0)))
    wo = w_gla_out.astype(BF16)

    bp = x_prompt.shape[0]
    hist0 = jnp.zeros((1, bp, POOL_HIST, D_MODEL), F32)
    s00 = jnp.zeros((1, bp, GLA_HEADS, GLA_DK, GLA_DV), F32)

    def trunk(x, p, hist, s0, start, pool_tile, gla_tile):
        h1, pool_state = _pool_layer(x, p, 0, hist, vec0, wpool, *ffn, start=start, **pool_tile)
        y, gla_state = _gla_layer(h1, p, 1, s0, vec1, win, wgr, wgu, wo, *ffn, **gla_tile)
        return y, pool_state, gla_state

    y_p, ps_p, gs_p = trunk(x_prompt, p_prompt, hist0, s00, 0,
                            dict(nb=1, tm=POOL_PROMPT_TILE, sub=PROMPT_SUB),
                            dict(nb=1, tm=GLA_PROMPT_TILE, sub=PROMPT_SUB))
    bs, ts = x_sample.shape[:2]
    sample_tile = dict(nb=bs, tm=ts, sub=ts)
    y_s, ps_s, gs_s = trunk(x_sample, p_sample, state_pool, state_gla, PAST_LEN,
                            sample_tile, sample_tile)
    return (y_p, y_s, ps_p, ps_s, gs_p, gs_s)
```
